```python
import jax, jax.numpy as jnp
from jax import lax
import numpy as np

D_MODEL = 1024
BATCH = 2
SEQ = 16384
DEPTH = 1
DEC_BATCH = 16
DEC_SEQ = 4096
PAST_LEN = 128

N_HEADS = 8
Q_RANK = 256
KV_RANK = 128
QK_NOPE = 64
QK_ROPE = 32
V_DIM = 64
ATTN_WIDTH = N_HEADS * V_DIM
ROPE_THETA = 10000.0
Q_BLOCK = 128
F_GROUPS = 4
F_GROUP_DIM = 128
F_WIDTH = F_GROUPS * F_GROUP_DIM
IN_COLS = Q_RANK + KV_RANK + QK_ROPE + F_WIDTH + 2 * D_MODEL
D_FF = 2816
CONV_W = 3
EPS = 1e-6

kernel_name = "hybrid_mla_fnet_convffn_encoder"


def _rms(x, g):
    xf = x.astype(jnp.float32)
    r = lax.rsqrt(jnp.mean(xf * xf, axis=-1, keepdims=True) + EPS)
    return (xf * r).astype(x.dtype) * g


def _rope(x, cos, sin):
    half = x.shape[-1] // 2
    xf = x.astype(jnp.float32)
    x1, x2 = xf[..., :half], xf[..., half:]
    out = jnp.concatenate([x1 * cos - x2 * sin, x2 * cos + x1 * sin], axis=-1)
    return out.astype(x.dtype)


def _mla_attention(q_nope, q_rope, k_nope, k_rope, v):
    B, S, H, _ = q_nope.shape
    nb = S // Q_BLOCK
    scale = (QK_NOPE + QK_ROPE) ** -0.5
    qn = q_nope.reshape(B, nb, Q_BLOCK, H, QK_NOPE).transpose(1, 0, 2, 3, 4)
    qr = q_rope.reshape(B, nb, Q_BLOCK, H, QK_ROPE).transpose(1, 0, 2, 3, 4)

    def one_block(args):
        qn_b, qr_b = args
        s = (jnp.einsum('bqhd,bkhd->bhqk', qn_b, k_nope)
             + jnp.einsum('bqhr,bkr->bhqk', qr_b, k_rope))
        p = jax.nn.softmax(s.astype(jnp.float32) * scale, axis=-1)
        return jnp.einsum('bhqk,bkhd->bqhd', p.astype(v.dtype), v)

    o = lax.map(one_block, (qn, qr))
    return o.transpose(1, 0, 2, 3, 4).reshape(B, S, H * V_DIM)


def _fourier(xf):
    B, S, _ = xf.shape
    g = xf.reshape(B, S, F_GROUPS, F_GROUP_DIM).astype(jnp.float32)
    f = jnp.fft.fft2(g, axes=(1, 3), norm='ortho').real
    return f.reshape(B, S, F_WIDTH).astype(xf.dtype)


def _dwconv3(u, w, b):
    up = jnp.pad(u, ((0, 0), (1, 1), (0, 0)))
    return up[:, :-2] * w[0] + up[:, 1:-1] * w[1] + up[:, 2:] * w[2] + b


def _layer(x, g_mix, w_in, g_q, w_uq, g_kv, w_ukv, w_attn_out, w_fourier_out,
           w_out, g_ffn, w_up, conv_w, conv_b, w_down):
    B, S, _ = x.shape
    h = _rms(x, g_mix)
    proj = h @ w_in
    c_q, c_kv, k_rope, xf, gates = jnp.split(
        proj, [Q_RANK, Q_RANK + KV_RANK, Q_RANK + KV_RANK + QK_ROPE,
               Q_RANK + KV_RANK + QK_ROPE + F_WIDTH], axis=-1)

    q = (_rms(c_q, g_q) @ w_uq).reshape(B, S, N_HEADS, QK_NOPE + QK_ROPE)
    q_nope, q_rope = q[..., :QK_NOPE], q[..., QK_NOPE:]
    kv = (_rms(c_kv, g_kv) @ w_ukv).reshape(B, S, N_HEADS, QK_NOPE + V_DIM)
    k_nope, v = kv[..., :QK_NOPE], kv[..., QK_NOPE:]
    pos = jnp.arange(S, dtype=jnp.float32)
    inv_freq = ROPE_THETA ** (-jnp.arange(0, QK_ROPE, 2, dtype=jnp.float32) / QK_ROPE)
    ang = pos[:, None] * inv_freq[None, :]
    cos, sin = jnp.cos(ang), jnp.sin(ang)
    q_rope = _rope(q_rope, cos[:, None, :], sin[:, None, :])
    k_rope = _rope(k_rope, cos, sin)
    a_branch = _mla_attention(q_nope, q_rope, k_nope, k_rope, v) @ w_attn_out

    f_branch = _fourier(xf) @ w_fourier_out

    g = jax.nn.sigmoid(gates)
    g_a, g_f = g[..., :D_MODEL], g[..., D_MODEL:]
    x = x + (g_a * a_branch + g_f * f_branch) @ w_out

    h2 = _rms(x, g_ffn)
    u = _dwconv3(h2 @ w_up, conv_w, conv_b)
    x = x + (jax.nn.silu(u[..., :D_FF]) * u[..., D_FF:]) @ w_down
    return x


def _trunk(x, g_mix, w_in, g_q, w_uq, g_kv, w_ukv, w_attn_out, w_fourier_out,
           w_out, g_ffn, w_up, conv_w, conv_b, w_down, g_final):
    for l in range(DEPTH):
        x = _layer(x, g_mix[l], w_in[l], g_q[l], w_uq[l], g_kv[l], w_ukv[l],
                   w_attn_out[l], w_fourier_out[l], w_out[l], g_ffn[l], w_up[l],
                   conv_w[l], conv_b[l], w_down[l])
    return _rms(x, g_final)


def setup_inputs(seed: int = 0) -> dict:
    key = jax.random.key(seed)
    ks = jax.random.split(key, 20)
    f32 = jnp.float32

    def w(k, shape, fan_in):
        return jax.random.normal(k, shape, f32) * (fan_in ** -0.5)

    def gain(k, shape):
        return 1.0 + 0.02 * jax.random.normal(k, shape, f32)

    L = DEPTH
    return {
        "x_prompt": jax.random.normal(ks[0], (BATCH, SEQ, D_MODEL), f32),
        "x_sample": jax.random.normal(ks[1], (DEC_BATCH, DEC_SEQ, D_MODEL), f32),
        "g_mix": gain(ks[2], (L, D_MODEL)),
        "w_in": w(ks[3], (L, D_MODEL, IN_COLS), D_MODEL),
        "g_q": gain(ks[4], (L, Q_RANK)),
        "w_uq": w(ks[5], (L, Q_RANK, N_HEADS * (QK_NOPE + QK_ROPE)), Q_RANK),
        "g_kv": gain(ks[6], (L, KV_RANK)),
        "w_ukv": w(ks[7], (L, KV_RANK, N_HEADS * (QK_NOPE + V_DIM)), KV_RANK),
        "w_attn_out": w(ks[8], (L, ATTN_WIDTH, D_MODEL), ATTN_WIDTH),
        "w_fourier_out": w(ks[9], (L, F_WIDTH, D_MODEL), F_WIDTH),
        "w_out": w(ks[10], (L, D_MODEL, D_MODEL), D_MODEL),
        "g_ffn": gain(ks[11], (L, D_MODEL)),
        "w_up": w(ks[12], (L, D_MODEL, 2 * D_FF), D_MODEL),
        "conv_w": w(ks[13], (L, CONV_W, 2 * D_FF), CONV_W),
        "conv_b": 0.02 * jax.random.normal(ks[14], (L, 2 * D_FF), f32),
        "w_down": w(ks[15], (L, D_FF, D_MODEL), D_FF),
        "g_final": gain(ks[16], (D_MODEL,)),
    }


def reference(x_prompt, x_sample, g_mix, w_in, g_q, w_uq, g_kv, w_ukv, w_attn_out,
              w_fourier_out, w_out, g_ffn, w_up, conv_w, conv_b, w_down, g_final):
    y_prompt = _trunk(x_prompt, g_mix, w_in, g_q, w_uq, g_kv, w_ukv, w_attn_out,
                      w_fourier_out, w_out, g_ffn, w_up, conv_w, conv_b, w_down, g_final)
    y_sample = _trunk(x_sample, g_mix, w_in, g_q, w_uq, g_kv, w_ukv, w_attn_out,
                      w_fourier_out, w_out, g_ffn, w_up, conv_w, conv_b, w_down, g_final)
    return (y_prompt, y_sample)
```

```python
import functools
import math

import jax
import jax.numpy as jnp
import numpy as np
from jax import lax
from jax.experimental import pallas as pl
from jax.experimental.pallas import tpu as pltpu

F32 = jnp.float32
BF16 = jnp.bfloat16

N_HEADS = 8
QK_NOPE = 64
QK_ROPE = 32
V_DIM = 64
HEAD_PAD = 128
F_GROUPS = 4
F_GROUP_DIM = 128
ROPE_THETA = 10000.0
EPS = 1e-6

LANES = 128
VMEM_LIMIT = 56 * 1024 * 1024


def _params(sem, vmem=VMEM_LIMIT):
    return pltpu.CompilerParams(dimension_semantics=sem, vmem_limit_bytes=vmem)


def _const_spec(shape):
    nd = len(shape)
    return pl.BlockSpec(shape, lambda *_: (0,) * nd)


def _rms(v, g):
    r = lax.rsqrt(jnp.mean(v * v, axis=-1, keepdims=True) + EPS)
    return v * r * g


def _dot(a, b):
    return jnp.dot(a, b, preferred_element_type=F32)


def _inproj_kernel(x_ref, ct_ref, st_ref, ck_ref, gmix_ref, win_ref, gq_ref, wq_ref,
                   gkv_ref, wkv_ref, wk_ref, dft_ref,
                   q_ref, k_ref, kv_ref, yr_ref, yi_ref, gate_ref, *, q_rank, kv_rank, f_width):
    h = _rms(x_ref[...], gmix_ref[...]).astype(BF16)
    proj = _dot(h, win_ref[...])
    o_kv = q_rank
    o_kr = o_kv + kv_rank
    o_f = o_kr + LANES
    o_g = o_f + f_width
    cqn = _rms(proj[:, :o_kv], gq_ref[...]).astype(BF16)
    ckvn = _rms(proj[:, o_kv:o_kr], gkv_ref[...])

    q2 = _dot(cqn, wq_ref[...])
    ct = ct_ref[...]
    st = st_ref[...]
    hw = N_HEADS * HEAD_PAD
    for hd in range(N_HEADS):
        lo = hd * HEAD_PAD
        q_ref[:, lo:lo + HEAD_PAD] = (q2[:, lo:lo + HEAD_PAD] * ct
                                      + q2[:, hw + lo:hw + lo + HEAD_PAD] * st).astype(BF16)

    kv_ref[...] = _dot(ckvn.astype(BF16), wkv_ref[...]).astype(BF16)

    kr = proj[:, o_kr:o_f] * ck_ref[...]
    kin = jnp.concatenate([ckvn, kr], axis=1).astype(BF16)
    k_ref[...] = _dot(kin, wk_ref[...]).astype(BF16)

    xf = proj[:, o_f:o_g].astype(BF16)
    for g in range(F_GROUPS):
        lo = g * F_GROUP_DIM
        y = _dot(xf[:, lo:lo + F_GROUP_DIM], dft_ref[...])
        yr_ref[:, lo:lo + F_GROUP_DIM] = y[:, :F_GROUP_DIM].astype(BF16)
        yi_ref[:, lo:lo + F_GROUP_DIM] = y[:, F_GROUP_DIM:].astype(BF16)

    gate_ref[...] = jax.nn.sigmoid(proj[:, o_g:]).astype(BF16)


def _inproj(x2d, seq, tabs, w, tm):
    n, d = x2d.shape
    spt = seq // tm
    q_rank = w["g_q"].shape[1]
    kv_rank = w["g_kv"].shape[1]
    f_width = F_GROUPS * F_GROUP_DIM
    hw = N_HEADS * HEAD_PAD
    n_gate = w["w_in"].shape[1] - (q_rank + kv_rank + LANES + f_width)
    row = lambda width: pl.BlockSpec((tm, width), lambda i: (i, 0))
    tab = pl.BlockSpec((tm, LANES), lambda i: (i % spt, 0))
    kern = functools.partial(_inproj_kernel, q_rank=q_rank, kv_rank=kv_rank, f_width=f_width)
    return pl.pallas_call(
        kern,
        grid=(n // tm,),
        in_specs=[row(d), tab, tab, tab,
                  _const_spec(w["g_mix"].shape), _const_spec(w["w_in"].shape),
                  _const_spec(w["g_q"].shape), _const_spec(w["w_q"].shape),
                  _const_spec(w["g_kv"].shape), _const_spec(w["w_kv"].shape),
                  _const_spec(w["w_k"].shape), _const_spec(tabs["dft_c"].shape)],
        out_specs=[row(hw), row(hw), row(hw), row(f_width), row(f_width), row(n_gate)],
        out_shape=[jax.ShapeDtypeStruct((n, hw), BF16), jax.ShapeDtypeStruct((n, hw), BF16),
                   jax.ShapeDtypeStruct((n, hw), BF16), jax.ShapeDtypeStruct((n, f_width), BF16),
                   jax.ShapeDtypeStruct((n, f_width), BF16), jax.ShapeDtypeStruct((n, n_gate), BF16)],
        compiler_params=_params(("parallel",)),
        name="inproj",
    )(x2d, tabs["ct"], tabs["st"], tabs["ck"], w["g_mix"], w["w_in"], w["g_q"], w["w_q"],
      w["g_kv"], w["w_kv"], w["w_k"], tabs["dft_c"])


def _attn_kernel(q_ref, k_ref, v_ref, o_ref, m_ref, l_ref, acc_ref, *, tk, n_kv):
    tq = q_ref.shape[0]
    outs = []
    for hh in range(2):
        lo = hh * HEAD_PAD
        q = q_ref[:, lo:lo + HEAD_PAD]
        m_ref[...] = jnp.full(m_ref.shape, -jnp.inf, F32)
        l_ref[...] = jnp.zeros(l_ref.shape, F32)
        acc_ref[...] = jnp.zeros(acc_ref.shape, F32)

        def body(j, carry, q=q, lo=lo):
            start = pl.multiple_of(j * tk, tk)
            k = k_ref[pl.ds(start, tk), lo:lo + HEAD_PAD]
            v = v_ref[pl.ds(start, tk), lo:lo + HEAD_PAD]
            s = lax.dot_general(q, k, (((1,), (1,)), ((), ())), preferred_element_type=F32)
            m_prev = m_ref[...]
            m_new = jnp.maximum(m_prev, jnp.max(s, axis=1, keepdims=True))
            alpha = jnp.exp(m_prev - m_new)
            p = jnp.exp(s - m_new[:, :1])
            l_ref[...] = alpha * l_ref[...] + jnp.sum(p, axis=1, keepdims=True)
            acc_ref[...] = alpha * acc_ref[...] + _dot(p.astype(BF16), v)
            m_ref[...] = m_new
            return carry

        lax.fori_loop(0, n_kv, body, 0)
        outs.append(acc_ref[...] / l_ref[...])
    lane = lax.broadcasted_iota(jnp.int32, (tq, HEAD_PAD), 1)
    o = jnp.where(lane < V_DIM, pltpu.roll(outs[0], V_DIM, axis=1), outs[1])
    o_ref[...] = o.astype(o_ref.dtype)


def _attention(qp, kp, kvp, batch, seq, tq, tk):
    n = qp.shape[0]
    nq = seq // tq
    pair = 2 * HEAD_PAD
    kern = functools.partial(_attn_kernel, tk=tk, n_kv=seq // tk)
    return pl.pallas_call(
        kern,
        grid=(batch, N_HEADS // 2, nq),
        in_specs=[pl.BlockSpec((tq, pair), lambda b, hp, i: (b * nq + i, hp)),
                  pl.BlockSpec((seq, pair), lambda b, hp, i: (b, hp)),
                  pl.BlockSpec((seq, pair), lambda b, hp, i: (b, hp))],
        out_specs=pl.BlockSpec((tq, HEAD_PAD), lambda b, hp, i: (b * nq + i, hp)),
        out_shape=jax.ShapeDtypeStruct((n, N_HEADS * V_DIM), BF16),
        scratch_shapes=[pltpu.VMEM((tq, HEAD_PAD), F32), pltpu.VMEM((tq, HEAD_PAD), F32),
                        pltpu.VMEM((tq, HEAD_PAD), F32)],
        compiler_params=_params(("parallel", "parallel", "arbitrary")),
        name="attention",
    )(qp, kp, kvp)


def _dft1_kernel(yr_ref, yi_ref, m1_ref, tr_ref, ti_ref):
    n1 = yr_ref.shape[0]
    y = jnp.concatenate([yr_ref[...], yi_ref[...]], axis=0)
    t = _dot(m1_ref[...], y)
    tr_ref[...] = t[:n1].astype(BF16)
    ti_ref[...] = t[n1:].astype(BF16)


def _dft1(yr, yi, m1, batch, n1, n2, cols):
    width = yr.shape[1]
    span = n2 * width
    yr3 = yr.reshape(batch, n1, span)
    yi3 = yi.reshape(batch, n1, span)
    blk = pl.BlockSpec((None, n1, cols), lambda b, j: (b, 0, j))
    return pl.pallas_call(
        _dft1_kernel,
        grid=(batch, span // cols),
        in_specs=[blk, blk, _const_spec(m1.shape)],
        out_specs=[blk, blk],
        out_shape=[jax.ShapeDtypeStruct((batch, n1, span), BF16)] * 2,
        compiler_params=_params(("parallel", "parallel")),
        name="seq_dft_stage1",
    )(yr3, yi3, m1)


def _dft2_kernel(tr_ref, ti_ref, g_ref, o_ref, *, cb, norm):
    for j in range(cb):
        t = jnp.concatenate([tr_ref[j], ti_ref[j]], axis=0)
        o_ref[:, j, :] = _dot(g_ref[j], t) * norm


def _dft2(tr, ti, gtab, batch, n1, n2, width, cb, norm):
    tr4 = tr.reshape(batch, n1, n2, width)
    ti4 = ti.reshape(batch, n1, n2, width)
    blk = pl.BlockSpec((None, cb, n2, width), lambda b, c: (b, c, 0, 0))
    kern = functools.partial(_dft2_kernel, cb=cb, norm=norm)
    out = pl.pallas_call(
        kern,
        grid=(batch, n1 // cb),
        in_specs=[blk, blk, pl.BlockSpec((cb, n2, 2 * n2), lambda b, c: (c, 0, 0))],
        out_specs=pl.BlockSpec((None, n2, cb, width), lambda b, c: (b, 0, c, 0)),
        out_shape=jax.ShapeDtypeStruct((batch, n2, n1, width), F32),
        compiler_params=_params(("parallel", "parallel")),
        name="seq_dft_stage2",
    )(tr4, ti4, gtab)
    return out.reshape(batch * n1 * n2, width)


def _merge_kernel(x_ref, ao_ref, f_ref, g_ref, wa_ref, wf_ref, wo_ref, o_ref):
    d = x_ref.shape[1]
    a = _dot(ao_ref[...], wa_ref[...])
    f = _dot(f_ref[...].astype(BF16), wf_ref[...])
    g = g_ref[...].astype(F32)
    mix = (g[:, :d] * a + g[:, d:] * f).astype(BF16)
    o_ref[...] = x_ref[...] + _dot(mix, wo_ref[...])


def _merge(x2d, ao, fo, gates, w, tm):
    n, d = x2d.shape
    row = lambda width: pl.BlockSpec((tm, width), lambda i: (i, 0))
    return pl.pallas_call(
        _merge_kernel,
        grid=(n // tm,),
        in_specs=[row(d), row(ao.shape[1]), row(fo.shape[1]), row(gates.shape[1]),
                  _const_spec(w["w_attn_out"].shape), _const_spec(w["w_fourier_out"].shape),
                  _const_spec(w["w_out"].shape)],
        out_specs=row(d),
        out_shape=jax.ShapeDtypeStruct((n, d), F32),
        compiler_params=_params(("parallel",)),
        name="merge",
    )(x2d, ao, fo, gates, w["w_attn_out"], w["w_fourier_out"], w["w_out"])


HALO = 8


def _ffn_kernel(x_ref, xp_ref, xn_ref, gffn_ref, wup_ref, cw_ref, cb_ref, wdn_ref, gfin_ref,
                o_ref, hext_ref, act_ref, *, spt, fc):
    tm = x_ref.shape[0]
    n_chunks = wup_ref.shape[0]
    i = pl.program_id(0)
    has_prev = (i % spt != 0).astype(F32)
    has_next = (i % spt != spt - 1).astype(F32)
    g = gffn_ref[...]
    x = x_ref[...]
    hext_ref[0:HALO, :] = (_rms(xp_ref[...], g) * has_prev).astype(BF16)
    hext_ref[HALO:HALO + tm, :] = _rms(x, g).astype(BF16)
    hext_ref[HALO + tm:, :] = (_rms(xn_ref[...], g) * has_next).astype(BF16)
    hext = hext_ref[...]
    rows = tm + 2 * HALO
    for c in range(n_chunks):
        u = _dot(hext, wup_ref[c])
        cw = cw_ref[c]
        conv = (pltpu.roll(u, 1, axis=0) * cw[0:1] + u * cw[1:2]
                + pltpu.roll(u, rows - 1, axis=0) * cw[2:3] + cb_ref[c])
        conv = conv[HALO:HALO + tm]
        gate = conv[:, :fc]
        act_ref[:, c * fc:(c + 1) * fc] = (gate * jax.nn.sigmoid(gate) * conv[:, fc:]).astype(BF16)
    x2 = x + _dot(act_ref[...], wdn_ref[...])
    o_ref[...] = _rms(x2, gfin_ref[...])


def _ffn(x1, seq, w, tm):
    n, d = x1.shape
    spt = seq // tm
    n_chunks, _, fc2 = w["w_up"].shape
    fc = fc2 // 2
    d_ff = n_chunks * fc
    hb = tm // HALO
    last = n // HALO - 1
    kern = functools.partial(_ffn_kernel, spt=spt, fc=fc)
    return pl.pallas_call(
        kern,
        grid=(n // tm,),
        in_specs=[pl.BlockSpec((tm, d), lambda i: (i, 0)),
                  pl.BlockSpec((HALO, d), lambda i: (jnp.maximum(i * hb - 1, 0), 0)),
                  pl.BlockSpec((HALO, d), lambda i: (jnp.minimum((i + 1) * hb, last), 0)),
                  _const_spec(w["g_ffn"].shape), _const_spec(w["w_up"].shape),
                  _const_spec(w["conv_w"].shape), _const_spec(w["conv_b"].shape),
                  _const_spec(w["w_down"].shape), _const_spec(w["g_final"].shape)],
        out_specs=pl.BlockSpec((tm, d), lambda i: (i, 0)),
        out_shape=jax.ShapeDtypeStruct((n, d), F32),
        scratch_shapes=[pltpu.VMEM((tm + 2 * HALO, d), BF16), pltpu.VMEM((tm, d_ff), BF16)],
        compiler_params=_params(("parallel",)),
        name="channel_mixer",
    )(x1, x1, x1, w["g_ffn"], w["w_up"], w["conv_w"], w["conv_b"], w["w_down"], w["g_final"])


FFN_CHUNK = 256


def _prep_weights(g_mix, w_in, g_q, w_uq, g_kv, w_ukv, w_attn_out, w_fourier_out, w_out,
                  g_ffn, w_up, conv_w, conv_b, w_down, g_final):
    q_rank = g_q.shape[0]
    kv_rank = g_kv.shape[0]
    d = w_in.shape[0]
    half = QK_ROPE // 2
    o_kr = q_rank + kv_rank
    o_f = o_kr + QK_ROPE
    kr = w_in[:, o_kr:o_f]
    kr_swapped = jnp.concatenate([kr[:, half:], kr[:, :half]], axis=1)
    kr_pad = jnp.zeros((d, LANES - 2 * QK_ROPE), F32)
    w_in_r = jnp.concatenate([w_in[:, :o_kr], kr, kr_swapped, kr_pad, w_in[:, o_f:]], axis=1)

    qh = w_uq.reshape(q_rank, N_HEADS, QK_NOPE + QK_ROPE)
    zq = jnp.zeros((q_rank, N_HEADS, HEAD_PAD - QK_NOPE - QK_ROPE), F32)
    w_q_plain = jnp.concatenate([qh, zq], axis=2)
    w_q_swap = jnp.concatenate([jnp.zeros((q_rank, N_HEADS, QK_NOPE), F32),
                                qh[:, :, QK_NOPE + half:], qh[:, :, QK_NOPE:QK_NOPE + half], zq], axis=2)
    w_q = jnp.concatenate([w_q_plain.reshape(q_rank, -1), w_q_swap.reshape(q_rank, -1)], axis=1)

    kvh = w_ukv.reshape(kv_rank, N_HEADS, QK_NOPE + V_DIM)
    w_k_top = jnp.concatenate([kvh[:, :, :QK_NOPE],
                               jnp.zeros((kv_rank, N_HEADS, HEAD_PAD - QK_NOPE), F32)], axis=2)
    eye = jnp.eye(QK_ROPE, dtype=F32)
    place = jnp.concatenate([jnp.zeros((QK_ROPE, QK_NOPE), F32), eye,
                             jnp.zeros((QK_ROPE, HEAD_PAD - QK_NOPE - QK_ROPE), F32)], axis=1)
    place = jnp.tile(place[:, None, :], (1, N_HEADS, 1))
    w_k_bot = jnp.concatenate([place, place,
                               jnp.zeros((LANES - 2 * QK_ROPE, N_HEADS, HEAD_PAD), F32)], axis=0)
    w_k = jnp.concatenate([w_k_top.reshape(kv_rank, -1), w_k_bot.reshape(LANES, -1)], axis=0)

    d_ff = w_down.shape[0]
    nc = d_ff // FFN_CHUNK
    gate_w = w_up[:, :d_ff].reshape(d, nc, FFN_CHUNK)
    val_w = w_up[:, d_ff:].reshape(d, nc, FFN_CHUNK)
    w_up_r = jnp.concatenate([gate_w, val_w], axis=2).transpose(1, 0, 2)
    k3 = conv_w.shape[0]
    cw_r = jnp.concatenate([conv_w[:, :d_ff].reshape(k3, nc, FFN_CHUNK),
                            conv_w[:, d_ff:].reshape(k3, nc, FFN_CHUNK)], axis=2).transpose(1, 0, 2)
    cb_r = jnp.concatenate([conv_b[:d_ff].reshape(nc, 1, FFN_CHUNK),
                            conv_b[d_ff:].reshape(nc, 1, FFN_CHUNK)], axis=2)
    return {
        "g_mix": g_mix[None, :], "w_in": w_in_r.astype(BF16),
        "g_q": g_q[None, :], "w_q": w_q.astype(BF16),
        "g_kv": g_kv[None, :], "w_kv": w_ukv.astype(BF16), "w_k": w_k.astype(BF16),
        "w_attn_out": w_attn_out.astype(BF16), "w_fourier_out": w_fourier_out.astype(BF16),
        "w_out": w_out.astype(BF16),
        "g_ffn": g_ffn[None, :], "w_up": w_up_r.astype(BF16), "conv_w": cw_r, "conv_b": cb_r,
        "w_down": w_down.astype(BF16), "g_final": g_final[None, :],
    }


def _dft_angles(rows, cols, period):
    prod = (rows.astype(jnp.int32) * cols.astype(jnp.int32)) % period
    return prod.astype(F32) * (2.0 * math.pi / period)


def _split_seq(seq):
    n1 = 1 << ((seq.bit_length() - 1 + 1) // 2)
    assert seq % n1 == 0
    return n1, seq // n1


def _tables(seq):
    half = QK_ROPE // 2
    scale = (QK_NOPE + QK_ROPE) ** -0.5
    pos = jnp.arange(seq, dtype=F32)
    inv_freq = ROPE_THETA ** (-jnp.arange(0, QK_ROPE, 2, dtype=F32) / QK_ROPE)
    ang = pos[:, None] * inv_freq[None, :]
    cos, sin = jnp.cos(ang), jnp.sin(ang)
    ones = jnp.ones((seq, QK_NOPE), F32)
    zeros = lambda width: jnp.zeros((seq, width), F32)
    pad = HEAD_PAD - QK_NOPE - QK_ROPE
    ct = scale * jnp.concatenate([ones, cos, cos, zeros(pad)], axis=1)
    st = scale * jnp.concatenate([zeros(QK_NOPE), -sin, sin, zeros(pad)], axis=1)
    ck = jnp.concatenate([cos, cos, -sin, sin, zeros(LANES - 2 * QK_ROPE)], axis=1)

    ch = jnp.arange(F_GROUP_DIM)
    ang_c = _dft_angles(ch[:, None], ch[None, :], F_GROUP_DIM)
    dft_c = jnp.concatenate([jnp.cos(ang_c), -jnp.sin(ang_c)], axis=1).astype(BF16)

    n1, n2 = _split_seq(seq)
    a = jnp.arange(n1)
    ang1 = _dft_angles(a[:, None], a[None, :], n1)
    c1, s1 = jnp.cos(ang1), jnp.sin(ang1)
    m1 = jnp.concatenate([jnp.concatenate([c1, s1], axis=1),
                          jnp.concatenate([-s1, c1], axis=1)], axis=0).astype(BF16)
    sp = (jnp.arange(n1)[:, None] + n1 * jnp.arange(n2)[None, :])
    ang2 = _dft_angles(sp[:, :, None], jnp.arange(n2)[None, None, :], seq)
    gtab = jnp.concatenate([jnp.cos(ang2), jnp.sin(ang2)], axis=2).astype(BF16)
    return {"ct": ct, "st": st, "ck": ck, "dft_c": dft_c, "m1": m1, "gtab": gtab,
            "n1": n1, "n2": n2}


def _pick(seq, pref):
    t = pref
    while seq % t:
        t //= 2
    return t


def _trunk(x, w):
    batch, seq, d = x.shape
    x2d = x.reshape(batch * seq, d)
    tabs = _tables(seq)
    n1, n2 = tabs["n1"], tabs["n2"]
    tm = _pick(seq, 512)
    qp, kp, kvp, yr, yi, gates = _inproj(x2d, seq, tabs, w, tm)
    ao = _attention(qp, kp, kvp, batch, seq, _pick(seq, 512), _pick(seq, 512))
    width = yr.shape[1]
    tr, ti = _dft1(yr, yi, tabs["m1"], batch, n1, n2, _pick(n2 * width, 4096))
    norm = 1.0 / math.sqrt(seq * F_GROUP_DIM)
    fo = _dft2(tr, ti, tabs["gtab"], batch, n1, n2, width, 8, norm)
    x1 = _merge(x2d, ao, fo, gates, w, tm)
    y = _ffn(x1, seq, w, tm)
    return y.reshape(batch, seq, d)


def kernel(x_prompt, x_sample, g_mix, w_in, g_q, w_uq, g_kv, w_ukv, w_attn_out, w_fourier_out,
           w_out, g_ffn, w_up, conv_w, conv_b, w_down, g_final):
    w = _prep_weights(g_mix[0], w_in[0], g_q[0], w_uq[0], g_kv[0], w_ukv[0], w_attn_out[0],
                      w_fourier_out[0], w_out[0], g_ffn[0], w_up[0], conv_w[0], conv_b[0],
                      w_down[0], g_final)
    return (_trunk(x_prompt, w), _trunk(x_sample, w))
```

```python
import functools
import math

import jax
import jax.numpy as jnp
import numpy as np
from jax import lax
from jax.experimental import pallas as pl
from jax.experimental.pallas import tpu as pltpu

F32 = jnp.float32
BF16 = jnp.bfloat16

N_HEADS = 8
QK_NOPE = 64
QK_ROPE = 32
V_DIM = 64
HEAD_PAD = 128
F_GROUPS = 4
F_GROUP_DIM = 128
ROPE_THETA = 10000.0
EPS = 1e-6

LANES = 128
VMEM_LIMIT = 56 * 1024 * 1024


def _params(sem, vmem=VMEM_LIMIT):
    return pltpu.CompilerParams(dimension_semantics=sem, vmem_limit_bytes=vmem)


def _const_spec(shape):
    nd = len(shape)
    return pl.BlockSpec(shape, lambda *_: (0,) * nd)


def _rms(v, g):
    r = lax.rsqrt(jnp.mean(v * v, axis=-1, keepdims=True) + EPS)
    return v * r * g


def _dot(a, b):
    return jnp.dot(a, b, preferred_element_type=F32)


def _inproj_kernel(x_ref, ct_ref, st_ref, ck_ref, gmix_ref, win_ref, gq_ref, wq_ref,
                   gkv_ref, wv_ref, wk_ref, dft_ref,
                   qt_ref, k_ref, vt_ref, yr_ref, yi_ref, gate_ref, *, q_rank, kv_rank, f_width):
    h = _rms(x_ref[...], gmix_ref[...]).astype(BF16)
    proj = _dot(h, win_ref[...])
    o_kv = q_rank
    o_kr = o_kv + kv_rank
    o_f = o_kr + LANES
    o_g = o_f + f_width
    cqn = _rms(proj[:, :o_kv], gq_ref[...]).astype(BF16)
    ckvn = _rms(proj[:, o_kv:o_kr], gkv_ref[...])

    q2 = _dot(cqn, wq_ref[...])
    ct = ct_ref[...]
    st = st_ref[...]
    hw = N_HEADS * HEAD_PAD
    q = jnp.concatenate(
        [q2[:, lo:lo + HEAD_PAD] * ct + q2[:, hw + lo:hw + lo + HEAD_PAD] * st
         for lo in range(0, hw, HEAD_PAD)], axis=1)
    qt_ref[...] = q.T.astype(BF16)

    lane = lax.broadcasted_iota(jnp.int32, (1, hw), 1)
    ones_col = (lane % HEAD_PAD == V_DIM).astype(F32)
    v = _dot(ckvn.astype(BF16), wv_ref[...]) + ones_col
    vt_ref[...] = v.T.astype(BF16)

    kr = proj[:, o_kr:o_f] * ck_ref[...]
    kin = jnp.concatenate([ckvn, kr], axis=1).astype(BF16)
    k_ref[...] = _dot(kin, wk_ref[...]).astype(BF16)

    xf = proj[:, o_f:o_g].astype(BF16)
    for g in range(F_GROUPS):
        lo = g * F_GROUP_DIM
        y = _dot(xf[:, lo:lo + F_GROUP_DIM], dft_ref[...])
        yr_ref[:, lo:lo + F_GROUP_DIM] = y[:, :F_GROUP_DIM].astype(BF16)
        yi_ref[:, lo:lo + F_GROUP_DIM] = y[:, F_GROUP_DIM:].astype(BF16)

    gate_ref[...] = jax.nn.sigmoid(proj[:, o_g:]).astype(BF16)


def _inproj(x2d, seq, tabs, w, tm):
    n, d = x2d.shape
    spt = seq // tm
    q_rank = w["g_q"].shape[1]
    kv_rank = w["g_kv"].shape[1]
    f_width = F_GROUPS * F_GROUP_DIM
    hw = N_HEADS * HEAD_PAD
    n_gate = w["w_in"].shape[1] - (q_rank + kv_rank + LANES + f_width)
    row = lambda width: pl.BlockSpec((tm, width), lambda i: (i, 0))
    tab = pl.BlockSpec((tm, LANES), lambda i: (i % spt, 0))
    tblk = pl.BlockSpec((None, None, hw, tm), lambda i: (i // spt, i % spt, 0, 0))
    t_shape = jax.ShapeDtypeStruct((n // seq, spt, hw, tm), BF16)
    kern = functools.partial(_inproj_kernel, q_rank=q_rank, kv_rank=kv_rank, f_width=f_width)
    return pl.pallas_call(
        kern,
        grid=(n // tm,),
        in_specs=[row(d), tab, tab, tab,
                  _const_spec(w["g_mix"].shape), _const_spec(w["w_in"].shape),
                  _const_spec(w["g_q"].shape), _const_spec(w["w_q"].shape),
                  _const_spec(w["g_kv"].shape), _const_spec(w["w_v"].shape),
                  _const_spec(w["w_k"].shape), _const_spec(tabs["dft_c"].shape)],
        out_specs=[tblk, row(hw), tblk, row(f_width), row(f_width), row(n_gate)],
        out_shape=[t_shape, jax.ShapeDtypeStruct((n, hw), BF16), t_shape,
                   jax.ShapeDtypeStruct((n, f_width), BF16),
                   jax.ShapeDtypeStruct((n, f_width), BF16), jax.ShapeDtypeStruct((n, n_gate), BF16)],
        compiler_params=_params(("parallel",)),
        name="inproj",
    )(x2d, tabs["ct"], tabs["st"], tabs["ck"], w["g_mix"], w["w_in"], w["g_q"], w["w_q"],
      w["g_kv"], w["w_v"], w["w_k"], tabs["dft_c"])


def _attn_kernel(qt_ref, k_ref, vt_ref, o_ref, m_ref, acc_ref, sa_ref, sb_ref, *, n_kv):
    m_ref[...] = jnp.full(m_ref.shape, -jnp.inf, F32)
    acc_ref[...] = jnp.zeros(acc_ref.shape, F32)

    def scores(j, s_ref):
        for hh in range(2):
            lo = hh * HEAD_PAD
            s_ref[hh] = _dot(k_ref[j, :, lo:lo + HEAD_PAD], qt_ref[lo:lo + HEAD_PAD, :])

    def update(j, s_ref):
        for hh in range(2):
            lo = hh * HEAD_PAD
            s = s_ref[hh]
            m_prev = m_ref[hh]
            m_new = jnp.maximum(m_prev, jnp.max(s, axis=0, keepdims=True))
            p = jnp.exp2(s - m_new).astype(BF16)
            pv = _dot(vt_ref[j, lo:lo + HEAD_PAD, :], p)
            acc_ref[hh] = jnp.exp2(m_prev - m_new) * acc_ref[hh] + pv
            m_ref[hh] = m_new

    def body(jj, carry):
        j = 2 * jj
        scores(j + 1, sb_ref)
        update(j, sa_ref)
        scores(j + 2, sa_ref)
        update(j + 1, sb_ref)
        return carry

    scores(0, sa_ref)
    lax.fori_loop(0, (n_kv - 1) // 2, body, 0)
    if n_kv % 2 == 0:
        scores(n_kv - 1, sb_ref)
        update(n_kv - 2, sa_ref)
        update(n_kv - 1, sb_ref)
    else:
        update(n_kv - 1, sa_ref)
    o_t = jnp.concatenate(
        [acc_ref[hh, :V_DIM, :] / acc_ref[hh, V_DIM:V_DIM + 1, :] for hh in range(2)], axis=0)
    o_ref[...] = o_t.T.astype(o_ref.dtype)


def _attention(qt, kp, vt, batch, seq, tile):
    n, hw = kp.shape
    nblk = seq // tile
    pair = 2 * HEAD_PAD
    k4 = kp.reshape(batch, nblk, tile, hw)
    kern = functools.partial(_attn_kernel, n_kv=nblk)
    return pl.pallas_call(
        kern,
        grid=(batch, N_HEADS // 2, nblk),
        in_specs=[pl.BlockSpec((None, None, pair, tile), lambda b, hp, i: (b, i, hp, 0)),
                  pl.BlockSpec((None, nblk, tile, pair), lambda b, hp, i: (b, 0, 0, hp)),
                  pl.BlockSpec((None, nblk, pair, tile), lambda b, hp, i: (b, 0, hp, 0))],
        out_specs=pl.BlockSpec((tile, HEAD_PAD), lambda b, hp, i: (b * nblk + i, hp)),
        out_shape=jax.ShapeDtypeStruct((n, N_HEADS * V_DIM), BF16),
        scratch_shapes=[pltpu.VMEM((2, 1, tile), F32), pltpu.VMEM((2, HEAD_PAD, tile), F32),
                        pltpu.VMEM((2, tile, tile), F32), pltpu.VMEM((2, tile, tile), F32)],
        compiler_params=_params(("parallel", "parallel", "arbitrary")),
        name="attention",
    )(qt, k4, vt)


def _dft1_kernel(yr_ref, yi_ref, m1_ref, tr_ref, ti_ref):
    n1 = yr_ref.shape[0]
    y = jnp.concatenate([yr_ref[...], yi_ref[...]], axis=0)
    t = _dot(m1_ref[...], y)
    tr_ref[...] = t[:n1].astype(BF16)
    ti_ref[...] = t[n1:].astype(BF16)


def _dft1(yr, yi, m1, batch, n1, n2, cols):
    width = yr.shape[1]
    span = n2 * width
    yr3 = yr.reshape(batch, n1, span)
    yi3 = yi.reshape(batch, n1, span)
    blk = pl.BlockSpec((None, n1, cols), lambda b, j: (b, 0, j))
    return pl.pallas_call(
        _dft1_kernel,
        grid=(batch, span // cols),
        in_specs=[blk, blk, _const_spec(m1.shape)],
        out_specs=[blk, blk],
        out_shape=[jax.ShapeDtypeStruct((batch, n1, span), BF16)] * 2,
        compiler_params=_params(("parallel", "parallel")),
        name="seq_dft_stage1",
    )(yr3, yi3, m1)


def _dft2_kernel(tr_ref, ti_ref, g_ref, o_ref, *, cb, norm):
    for j in range(cb):
        t = jnp.concatenate([tr_ref[j], ti_ref[j]], axis=0)
        o_ref[:, j, :] = _dot(g_ref[j], t) * norm


def _dft2(tr, ti, gtab, batch, n1, n2, width, cb, norm):
    tr4 = tr.reshape(batch, n1, n2, width)
    ti4 = ti.reshape(batch, n1, n2, width)
    blk = pl.BlockSpec((None, cb, n2, width), lambda b, c: (b, c, 0, 0))
    kern = functools.partial(_dft2_kernel, cb=cb, norm=norm)
    out = pl.pallas_call(
        kern,
        grid=(batch, n1 // cb),
        in_specs=[blk, blk, pl.BlockSpec((cb, n2, 2 * n2), lambda b, c: (c, 0, 0))],
        out_specs=pl.BlockSpec((None, n2, cb, width), lambda b, c: (b, 0, c, 0)),
        out_shape=jax.ShapeDtypeStruct((batch, n2, n1, width), F32),
        compiler_params=_params(("parallel", "parallel")),
        name="seq_dft_stage2",
    )(tr4, ti4, gtab)
    return out.reshape(batch * n1 * n2, width)


def _merge_kernel(x_ref, ao_ref, f_ref, g_ref, wa_ref, wf_ref, wo_ref, o_ref):
    d = x_ref.shape[1]
    a = _dot(ao_ref[...], wa_ref[...])
    f = _dot(f_ref[...].astype(BF16), wf_ref[...])
    g = g_ref[...].astype(F32)
    mix = (g[:, :d] * a + g[:, d:] * f).astype(BF16)
    o_ref[...] = x_ref[...] + _dot(mix, wo_ref[...])


def _merge(x2d, ao, fo, gates, w, tm):
    n, d = x2d.shape
    row = lambda width: pl.BlockSpec((tm, width), lambda i: (i, 0))
    return pl.pallas_call(
        _merge_kernel,
        grid=(n // tm,),
        in_specs=[row(d), row(ao.shape[1]), row(fo.shape[1]), row(gates.shape[1]),
                  _const_spec(w["w_attn_out"].shape), _const_spec(w["w_fourier_out"].shape),
                  _const_spec(w["w_out"].shape)],
        out_specs=row(d),
        out_shape=jax.ShapeDtypeStruct((n, d), F32),
        compiler_params=_params(("parallel",)),
        name="merge",
    )(x2d, ao, fo, gates, w["w_attn_out"], w["w_fourier_out"], w["w_out"])


HALO = 8


def _ffn_kernel(x_ref, xp_ref, xn_ref, gffn_ref, wup_ref, cw_ref, cb_ref, wdn_ref, gfin_ref,
                o_ref, hext_ref, act_ref, *, spt, fc):
    tm = x_ref.shape[0]
    n_chunks = wup_ref.shape[0]
    i = pl.program_id(0)
    has_prev = (i % spt != 0).astype(F32)
    has_next = (i % spt != spt - 1).astype(F32)
    g = gffn_ref[...]
    x = x_ref[...]
    hext_ref[0:HALO, :] = (_rms(xp_ref[...], g) * has_prev).astype(BF16)
    hext_ref[HALO:HALO + tm, :] = _rms(x, g).astype(BF16)
    hext_ref[HALO + tm:, :] = (_rms(xn_ref[...], g) * has_next).astype(BF16)
    hext = hext_ref[...]
    rows = tm + 2 * HALO
    for c in range(n_chunks):
        u = _dot(hext, wup_ref[c])
        cw = cw_ref[c]
        conv = (pltpu.roll(u, 1, axis=0) * cw[0:1] + u * cw[1:2]
                + pltpu.roll(u, rows - 1, axis=0) * cw[2:3] + cb_ref[c])
        conv = conv[HALO:HALO + tm]
        gate = conv[:, :fc]
        act_ref[:, c * fc:(c + 1) * fc] = (gate * jax.nn.sigmoid(gate) * conv[:, fc:]).astype(BF16)
    x2 = x + _dot(act_ref[...], wdn_ref[...])
    o_ref[...] = _rms(x2, gfin_ref[...])


def _ffn(x1, seq, w, tm):
    n, d = x1.shape
    spt = seq // tm
    n_chunks, _, fc2 = w["w_up"].shape
    fc = fc2 // 2
    d_ff = n_chunks * fc
    hb = tm // HALO
    last = n // HALO - 1
    kern = functools.partial(_ffn_kernel, spt=spt, fc=fc)
    return pl.pallas_call(
        kern,
        grid=(n // tm,),
        in_specs=[pl.BlockSpec((tm, d), lambda i: (i, 0)),
                  pl.BlockSpec((HALO, d), lambda i: (jnp.maximum(i * hb - 1, 0), 0)),
                  pl.BlockSpec((HALO, d), lambda i: (jnp.minimum((i + 1) * hb, last), 0)),
                  _const_spec(w["g_ffn"].shape), _const_spec(w["w_up"].shape),
                  _const_spec(w["conv_w"].shape), _const_spec(w["conv_b"].shape),
                  _const_spec(w["w_down"].shape), _const_spec(w["g_final"].shape)],
        out_specs=pl.BlockSpec((tm, d), lambda i: (i, 0)),
        out_shape=jax.ShapeDtypeStruct((n, d), F32),
        scratch_shapes=[pltpu.VMEM((tm + 2 * HALO, d), BF16), pltpu.VMEM((tm, d_ff), BF16)],
        compiler_params=_params(("parallel",)),
        name="channel_mixer",
    )(x1, x1, x1, w["g_ffn"], w["w_up"], w["conv_w"], w["conv_b"], w["w_down"], w["g_final"])


FFN_CHUNK = 256


def _prep_weights(g_mix, w_in, g_q, w_uq, g_kv, w_ukv, w_attn_out, w_fourier_out, w_out,
                  g_ffn, w_up, conv_w, conv_b, w_down, g_final):
    q_rank = g_q.shape[0]
    kv_rank = g_kv.shape[0]
    d = w_in.shape[0]
    half = QK_ROPE // 2
    o_kr = q_rank + kv_rank
    o_f = o_kr + QK_ROPE
    kr = w_in[:, o_kr:o_f]
    kr_swapped = jnp.concatenate([kr[:, half:], kr[:, :half]], axis=1)
    kr_pad = jnp.zeros((d, LANES - 2 * QK_ROPE), F32)
    w_in_r = jnp.concatenate([w_in[:, :o_kr], kr, kr_swapped, kr_pad, w_in[:, o_f:]], axis=1)

    qh = w_uq.reshape(q_rank, N_HEADS, QK_NOPE + QK_ROPE)
    zq = jnp.zeros((q_rank, N_HEADS, HEAD_PAD - QK_NOPE - QK_ROPE), F32)
    w_q_plain = jnp.concatenate([qh, zq], axis=2)
    w_q_swap = jnp.concatenate([jnp.zeros((q_rank, N_HEADS, QK_NOPE), F32),
                                qh[:, :, QK_NOPE + half:], qh[:, :, QK_NOPE:QK_NOPE + half], zq], axis=2)
    w_q = jnp.concatenate([w_q_plain.reshape(q_rank, -1), w_q_swap.reshape(q_rank, -1)], axis=1)

    kvh = w_ukv.reshape(kv_rank, N_HEADS, QK_NOPE + V_DIM)
    w_k_top = jnp.concatenate([kvh[:, :, :QK_NOPE],
                               jnp.zeros((kv_rank, N_HEADS, HEAD_PAD - QK_NOPE), F32)], axis=2)
    w_v = jnp.concatenate([kvh[:, :, QK_NOPE:],
                           jnp.zeros((kv_rank, N_HEADS, HEAD_PAD - V_DIM), F32)], axis=2)
    eye = jnp.eye(QK_ROPE, dtype=F32)
    place = jnp.concatenate([jnp.zeros((QK_ROPE, QK_NOPE), F32), eye,
                             jnp.zeros((QK_ROPE, HEAD_PAD - QK_NOPE - QK_ROPE), F32)], axis=1)
    place = jnp.tile(place[:, None, :], (1, N_HEADS, 1))
    w_k_bot = jnp.concatenate([place, place,
                               jnp.zeros((LANES - 2 * QK_ROPE, N_HEADS, HEAD_PAD), F32)], axis=0)
    w_k = jnp.concatenate([w_k_top.reshape(kv_rank, -1), w_k_bot.reshape(LANES, -1)], axis=0)

    d_ff = w_down.shape[0]
    nc = d_ff // FFN_CHUNK
    gate_w = w_up[:, :d_ff].reshape(d, nc, FFN_CHUNK)
    val_w = w_up[:, d_ff:].reshape(d, nc, FFN_CHUNK)
    w_up_r = jnp.concatenate([gate_w, val_w], axis=2).transpose(1, 0, 2)
    k3 = conv_w.shape[0]
    cw_r = jnp.concatenate([conv_w[:, :d_ff].reshape(k3, nc, FFN_CHUNK),
                            conv_w[:, d_ff:].reshape(k3, nc, FFN_CHUNK)], axis=2).transpose(1, 0, 2)
    cb_r = jnp.concatenate([conv_b[:d_ff].reshape(nc, 1, FFN_CHUNK),
                            conv_b[d_ff:].reshape(nc, 1, FFN_CHUNK)], axis=2)
    return {
        "g_mix": g_mix[None, :], "w_in": w_in_r.astype(BF16),
        "g_q": g_q[None, :], "w_q": w_q.astype(BF16),
        "g_kv": g_kv[None, :], "w_v": w_v.reshape(kv_rank, -1).astype(BF16), "w_k": w_k.astype(BF16),
        "w_attn_out": w_attn_out.astype(BF16), "w_fourier_out": w_fourier_out.astype(BF16),
        "w_out": w_out.astype(BF16),
        "g_ffn": g_ffn[None, :], "w_up": w_up_r.astype(BF16), "conv_w": cw_r, "conv_b": cb_r,
        "w_down": w_down.astype(BF16), "g_final": g_final[None, :],
    }


def _dft_angles(rows, cols, period):
    prod = (rows.astype(jnp.int32) * cols.astype(jnp.int32)) % period
    return prod.astype(F32) * (2.0 * math.pi / period)


def _split_seq(seq):
    n1 = 1 << ((seq.bit_length() - 1 + 1) // 2)
    assert seq % n1 == 0
    return n1, seq // n1


def _tables(seq):
    half = QK_ROPE // 2
    scale = (QK_NOPE + QK_ROPE) ** -0.5 * math.log2(math.e)
    pos =jnp.arange(seq, dtype=F32)
    inv_freq = ROPE_THETA ** (-jnp.arange(0, QK_ROPE, 2, dtype=F32) / QK_ROPE)
    ang = pos[:, None] * inv_freq[None, :]
    cos, sin = jnp.cos(ang), jnp.sin(ang)
    ones = jnp.ones((seq, QK_NOPE), F32)
    zeros = lambda width: jnp.zeros((seq, width), F32)
    pad = HEAD_PAD - QK_NOPE - QK_ROPE
    ct = scale * jnp.concatenate([ones, cos, cos, zeros(pad)], axis=1)
    st = scale * jnp.concatenate([zeros(QK_NOPE), -sin, sin, zeros(pad)], axis=1)
    ck = jnp.concatenate([cos, cos, -sin, sin, zeros(LANES - 2 * QK_ROPE)], axis=1)

    ch = jnp.arange(F_GROUP_DIM)
    ang_c = _dft_angles(ch[:, None], ch[None, :], F_GROUP_DIM)
    dft_c = jnp.concatenate([jnp.cos(ang_c), -jnp.sin(ang_c)], axis=1).astype(BF16)

    n1, n2 = _split_seq(seq)
    a = jnp.arange(n1)
    ang1 = _dft_angles(a[:, None], a[None, :], n1)
    c1, s1 = jnp.cos(ang1), jnp.sin(ang1)
    m1 = jnp.concatenate([jnp.concatenate([c1, s1], axis=1),
                          jnp.concatenate([-s1, c1], axis=1)], axis=0).astype(BF16)
    sp = (jnp.arange(n1)[:, None] + n1 * jnp.arange(n2)[None, :])
    ang2 = _dft_angles(sp[:, :, None], jnp.arange(n2)[None, None, :], seq)
    gtab = jnp.concatenate([jnp.cos(ang2), jnp.sin(ang2)], axis=2).astype(BF16)
    return {"ct": ct, "st": st, "ck": ck, "dft_c": dft_c, "m1": m1, "gtab": gtab,
            "n1": n1, "n2": n2}


def _pick(seq, pref):
    t = pref
    while seq % t:
        t //= 2
    return t


def _trunk(x, w):
    batch, seq, d = x.shape
    x2d = x.reshape(batch * seq, d)
    tabs = _tables(seq)
    n1, n2 = tabs["n1"], tabs["n2"]
    tm = _pick(seq, 512)
    qt, kp, vt, yr, yi, gates = _inproj(x2d, seq, tabs, w, tm)
    ao = _attention(qt, kp, vt, batch, seq, tm)
    width = yr.shape[1]
    tr, ti = _dft1(yr, yi, tabs["m1"], batch, n1, n2, _pick(n2 * width, 4096))
    norm = 1.0 / math.sqrt(seq * F_GROUP_DIM)
    fo = _dft2(tr, ti, tabs["gtab"], batch, n1, n2, width, 8, norm)
    x1 = _merge(x2d, ao, fo, gates, w, tm)
    y = _ffn(x1, seq, w, tm)
    return y.reshape(batch, seq, d)


def kernel(x_prompt, x_sample, g_mix, w_in, g_q, w_uq, g_kv, w_ukv, w_attn_out, w_fourier_out,
           w_out, g_ffn, w_up, conv_w, conv_b, w_down, g_final):
    w = _prep_weights(g_mix[0], w_in[0], g_q[0], w_uq[0], g_kv[0], w_ukv[0], w_attn_out[0],
                      w_fourier_out[0], w_out[0], g_ffn[0], w_up[0], conv_w[0], conv_b[0],
                      w_down[0], g_final)
    return (_trunk(x_prompt, w), _trunk(x_sample, w))
```

```python
import functools
import math

import jax
import jax.numpy as jnp
import numpy as np
from jax import lax
from jax.experimental import pallas as pl
from jax.experimental.pallas import tpu as pltpu

F32 = jnp.float32
BF16 = jnp.bfloat16

N_HEADS = 8
QK_NOPE = 64
QK_ROPE = 32
V_DIM = 64
HEAD_PAD = 128
F_GROUPS = 4
F_GROUP_DIM = 128
ROPE_THETA = 10000.0
EPS = 1e-6

LANES = 128
VMEM_LIMIT = 56 * 1024 * 1024


def _params(sem, vmem=VMEM_LIMIT):
    return pltpu.CompilerParams(dimension_semantics=sem, vmem_limit_bytes=vmem)


def _const_spec(shape):
    nd = len(shape)
    return pl.BlockSpec(shape, lambda *_: (0,) * nd)


def _rms(v, g):
    r = lax.rsqrt(jnp.mean(v * v, axis=-1, keepdims=True) + EPS)
    return v * r * g


def _dot(a, b):
    return jnp.dot(a, b, preferred_element_type=F32)


def _inproj_kernel(x_ref, ct_ref, st_ref, ck_ref, gmix_ref, win_ref, gq_ref, wq_ref,
                   gkv_ref, wv_ref, wk_ref, dft_ref,
                   qt_ref, k_ref, vt_ref, yr_ref, yi_ref, gate_ref, *, q_rank, kv_rank, f_width):
    h = _rms(x_ref[...], gmix_ref[...]).astype(BF16)
    proj = _dot(h, win_ref[...])
    o_kv = q_rank
    o_kr = o_kv + kv_rank
    o_f = o_kr + LANES
    o_g = o_f + f_width
    cqn = _rms(proj[:, :o_kv], gq_ref[...]).astype(BF16)
    ckvn = _rms(proj[:, o_kv:o_kr], gkv_ref[...])

    q2 = _dot(cqn, wq_ref[...])
    ct = ct_ref[...]
    st = st_ref[...]
    hw = N_HEADS * HEAD_PAD
    q = jnp.concatenate(
        [q2[:, lo:lo + HEAD_PAD] * ct + q2[:, hw + lo:hw + lo + HEAD_PAD] * st
         for lo in range(0, hw, HEAD_PAD)], axis=1)
    qt_ref[...] = q.T.astype(BF16)

    lane = lax.broadcasted_iota(jnp.int32, (1, hw), 1)
    ones_col = (lane % HEAD_PAD == V_DIM).astype(F32)
    v = _dot(ckvn.astype(BF16), wv_ref[...]) + ones_col
    vt_ref[...] = v.T.astype(BF16)

    kr = proj[:, o_kr:o_f] * ck_ref[...]
    kin = jnp.concatenate([ckvn, kr], axis=1).astype(BF16)
    k_ref[...] = _dot(kin, wk_ref[...]).astype(BF16)

    xf = proj[:, o_f:o_g].astype(BF16)
    for g in range(F_GROUPS):
        lo = g * F_GROUP_DIM
        y = _dot(xf[:, lo:lo + F_GROUP_DIM], dft_ref[...])
        yr_ref[:, lo:lo + F_GROUP_DIM] = y[:, :F_GROUP_DIM].astype(BF16)
        yi_ref[:, lo:lo + F_GROUP_DIM] = y[:, F_GROUP_DIM:].astype(BF16)

    gate_ref[...] = jax.nn.sigmoid(proj[:, o_g:]).astype(BF16)


def _inproj(x2d, seq, tabs, w, tm):
    n, d = x2d.shape
    spt = seq // tm
    q_rank = w["g_q"].shape[1]
    kv_rank = w["g_kv"].shape[1]
    f_width = F_GROUPS * F_GROUP_DIM
    hw = N_HEADS * HEAD_PAD
    n_gate = w["w_in"].shape[1] - (q_rank + kv_rank + LANES + f_width)
    row = lambda width: pl.BlockSpec((tm, width), lambda i: (i, 0))
    tab = pl.BlockSpec((tm, LANES), lambda i: (i % spt, 0))
    tblk = pl.BlockSpec((None, None, hw, tm), lambda i: (i // spt, i % spt, 0, 0))
    t_shape = jax.ShapeDtypeStruct((n // seq, spt, hw, tm), BF16)
    kern = functools.partial(_inproj_kernel, q_rank=q_rank, kv_rank=kv_rank, f_width=f_width)
    return pl.pallas_call(
        kern,
        grid=(n // tm,),
        in_specs=[row(d), tab, tab, tab,
                  _const_spec(w["g_mix"].shape), _const_spec(w["w_in"].shape),
                  _const_spec(w["g_q"].shape), _const_spec(w["w_q"].shape),
                  _const_spec(w["g_kv"].shape), _const_spec(w["w_v"].shape),
                  _const_spec(w["w_k"].shape), _const_spec(tabs["dft_c"].shape)],
        out_specs=[tblk, row(hw), tblk, row(f_width), row(f_width), row(n_gate)],
        out_shape=[t_shape, jax.ShapeDtypeStruct((n, hw), BF16), t_shape,
                   jax.ShapeDtypeStruct((n, f_width), BF16),
                   jax.ShapeDtypeStruct((n, f_width), BF16), jax.ShapeDtypeStruct((n, n_gate), BF16)],
        compiler_params=_params(("parallel",)),
        name="inproj",
    )(x2d, tabs["ct"], tabs["st"], tabs["ck"], w["g_mix"], w["w_in"], w["g_q"], w["w_q"],
      w["g_kv"], w["w_v"], w["w_k"], tabs["dft_c"])


BF16_SUBLANES = 16
PV_ROWS = V_DIM + BF16_SUBLANES
ATTN_UNROLL = 4


def _attn_kernel(qt_ref, k_ref, vt_ref, o_ref, m_ref, acc_ref, s_ref, smax_ref, *, n_kv):
    m_ref[...] = jnp.full(m_ref.shape, -jnp.inf, F32)
    acc_ref[...] = jnp.zeros(acc_ref.shape, F32)

    def scores(j, buf):
        for hh in range(2):
            lo = hh * HEAD_PAD
            s = _dot(k_ref[j, :, lo:lo + HEAD_PAD], qt_ref[lo:lo + HEAD_PAD, :])
            s_ref[buf, hh] = s
            smax_ref[buf, hh] = jnp.max(s, axis=0, keepdims=True)

    def update(j, buf):
        for hh in range(2):
            lo = hh * HEAD_PAD
            m_prev = m_ref[hh]
            m_new = jnp.maximum(m_prev, smax_ref[buf, hh])
            p = jnp.exp2(s_ref[buf, hh] - m_new).astype(BF16)
            pv = _dot(vt_ref[j, lo:lo + PV_ROWS, :], p)
            acc_ref[hh] = jnp.exp2(m_prev - m_new) * acc_ref[hh] + pv
            m_ref[hh] = m_new

    def run(first, count, last):
        for u in range(count):
            if not (last and u == count - 1):
                scores(first + u + 1, (u + 1) % ATTN_UNROLL)
            update(first + u, u)

    def body(jj, carry):
        run(ATTN_UNROLL * jj, ATTN_UNROLL, False)
        return carry

    trips = (n_kv - 1) // ATTN_UNROLL
    scores(0, 0)
    lax.fori_loop(0, trips, body, 0)
    run(ATTN_UNROLL * trips, n_kv - ATTN_UNROLL * trips, True)
    o_t = jnp.concatenate(
        [acc_ref[hh, :V_DIM, :] / acc_ref[hh, V_DIM:V_DIM + 1, :] for hh in range(2)], axis=0)
    o_ref[...] = o_t.T.astype(o_ref.dtype)


def _attention(qt, kp, vt, batch, seq, tile):
    n, hw = kp.shape
    nblk = seq // tile
    pair = 2 * HEAD_PAD
    k4 = kp.reshape(batch, nblk, tile, hw)
    kern = functools.partial(_attn_kernel, n_kv=nblk)
    return pl.pallas_call(
        kern,
        grid=(batch, N_HEADS // 2, nblk),
        in_specs=[pl.BlockSpec((None, None, pair, tile), lambda b, hp, i: (b, i, hp, 0)),
                  pl.BlockSpec((None, nblk, tile, pair), lambda b, hp, i: (b, 0, 0, hp)),
                  pl.BlockSpec((None, nblk, pair, tile), lambda b, hp, i: (b, 0, hp, 0))],
        out_specs=pl.BlockSpec((tile, HEAD_PAD), lambda b, hp, i: (b * nblk + i, hp)),
        out_shape=jax.ShapeDtypeStruct((n, N_HEADS * V_DIM), BF16),
        scratch_shapes=[pltpu.VMEM((2, 1, tile), F32), pltpu.VMEM((2, PV_ROWS, tile), F32),
                        pltpu.VMEM((ATTN_UNROLL, 2, tile, tile), F32),
                        pltpu.VMEM((ATTN_UNROLL, 2, 1, tile), F32)],
        compiler_params=_params(("parallel", "parallel", "arbitrary")),
        name="attention",
    )(qt, k4, vt)


DFT_ROWS = BF16_SUBLANES


def _dft1_kernel(yr_ref, yi_ref, m1_ref, t_ref, scr_ref):
    scr_ref[0] = yr_ref[...].astype(F32)
    scr_ref[1] = yi_ref[...].astype(F32)
    for b in range(DFT_ROWS):
        y = jnp.concatenate([scr_ref[0, :, b, :], scr_ref[1, :, b, :]], axis=0).astype(BF16)
        t_ref[b] = _dot(m1_ref[...], y).astype(BF16)


def _dft1(yr, yi, m1, batch, n1, n2):
    width = yr.shape[1]
    yr4 = yr.reshape(batch, n1, n2, width)
    yi4 = yi.reshape(batch, n1, n2, width)
    blk = pl.BlockSpec((None, n1, DFT_ROWS, width), lambda bt, j: (bt, 0, j, 0))
    return pl.pallas_call(
        _dft1_kernel,
        grid=(batch, n2 // DFT_ROWS),
        in_specs=[blk, blk, _const_spec(m1.shape)],
        out_specs=pl.BlockSpec((None, DFT_ROWS, 2 * n1, width), lambda bt, j: (bt, j, 0, 0)),
        out_shape=jax.ShapeDtypeStruct((batch, n2, 2 * n1, width), BF16),
        scratch_shapes=[pltpu.VMEM((2, n1, DFT_ROWS, width), F32)],
        compiler_params=_params(("parallel", "parallel")),
        name="seq_dft_stage1",
    )(yr4, yi4, m1)


def _dft2_kernel(tr_ref, ti_ref, g_ref, o_ref, scr_ref, oscr_ref, *, norm):
    scr_ref[0] = tr_ref[...].astype(F32)
    scr_ref[1] = ti_ref[...].astype(F32)
    for j in range(DFT_ROWS):
        t = jnp.concatenate([scr_ref[0, :, j, :], scr_ref[1, :, j, :]], axis=0).astype(BF16)
        oscr_ref[:, j, :] = _dot(g_ref[j], t) * norm
    o_ref[...] = oscr_ref[...].astype(o_ref.dtype)


def _dft2(t, gtab, batch, n1, n2, width, norm):
    cblks = n1 // DFT_ROWS
    kern = functools.partial(_dft2_kernel, norm=norm)
    out = pl.pallas_call(
        kern,
        grid=(batch, cblks),
        in_specs=[pl.BlockSpec((None, n2, DFT_ROWS, width), lambda bt, c: (bt, 0, c, 0)),
                  pl.BlockSpec((None, n2, DFT_ROWS, width), lambda bt, c: (bt, 0, cblks + c, 0)),
                  pl.BlockSpec((DFT_ROWS, n2, 2 * n2), lambda bt, c: (c, 0, 0))],
        out_specs=pl.BlockSpec((None, n2, DFT_ROWS, width), lambda bt, c: (bt, 0, c, 0)),
        out_shape=jax.ShapeDtypeStruct((batch, n2, n1, width), BF16),
        scratch_shapes=[pltpu.VMEM((2, n2, DFT_ROWS, width), F32),
                        pltpu.VMEM((n2, DFT_ROWS, width), F32)],
        compiler_params=_params(("parallel", "parallel")),
        name="seq_dft_stage2",
    )(t, t, gtab)
    return out.reshape(batch * n1 * n2, width)


def _merge_kernel(x_ref, ao_ref, f_ref, g_ref, wa_ref, wf_ref, wo_ref, o_ref):
    d = x_ref.shape[1]
    a = _dot(ao_ref[...], wa_ref[...])
    f = _dot(f_ref[...], wf_ref[...])
    g = g_ref[...].astype(F32)
    mix = (g[:, :d] * a + g[:, d:] * f).astype(BF16)
    o_ref[...] = x_ref[...] + _dot(mix, wo_ref[...])


def _merge(x2d, ao, fo, gates, w, tm):
    n, d = x2d.shape
    row = lambda width: pl.BlockSpec((tm, width), lambda i: (i, 0))
    return pl.pallas_call(
        _merge_kernel,
        grid=(n // tm,),
        in_specs=[row(d), row(ao.shape[1]), row(fo.shape[1]), row(gates.shape[1]),
                  _const_spec(w["w_attn_out"].shape), _const_spec(w["w_fourier_out"].shape),
                  _const_spec(w["w_out"].shape)],
        out_specs=row(d),
        out_shape=jax.ShapeDtypeStruct((n, d), F32),
        compiler_params=_params(("parallel",)),
        name="merge",
    )(x2d, ao, fo, gates, w["w_attn_out"], w["w_fourier_out"], w["w_out"])


HALO = 8


def _ffn_kernel(x_ref, xp_ref, xn_ref, gffn_ref, wup_ref, cw_ref, cb_ref, wdn_ref, gfin_ref,
                o_ref, hext_ref, act_ref, *, spt, fc):
    tm = x_ref.shape[0]
    n_chunks = wup_ref.shape[0]
    i = pl.program_id(0)
    has_prev = (i % spt != 0).astype(F32)
    has_next = (i % spt != spt - 1).astype(F32)
    g = gffn_ref[...]
    x = x_ref[...]
    hext_ref[0:HALO, :] = (_rms(xp_ref[...], g) * has_prev).astype(BF16)
    hext_ref[HALO:HALO + tm, :] = _rms(x, g).astype(BF16)
    hext_ref[HALO + tm:, :] = (_rms(xn_ref[...], g) * has_next).astype(BF16)
    hext = hext_ref[...]
    rows = tm + 2 * HALO
    for c in range(n_chunks):
        u = _dot(hext, wup_ref[c])
        cw = cw_ref[c]
        conv = (pltpu.roll(u, 1, axis=0) * cw[0:1] + u * cw[1:2]
                + pltpu.roll(u, rows - 1, axis=0) * cw[2:3] + cb_ref[c])
        conv = conv[HALO:HALO + tm]
        gate = conv[:, :fc]
        act_ref[:, c * fc:(c + 1) * fc] = (gate * jax.nn.sigmoid(gate) * conv[:, fc:]).astype(BF16)
    x2 = x + _dot(act_ref[...], wdn_ref[...])
    o_ref[...] = _rms(x2, gfin_ref[...])


def _ffn(x1, seq, w, tm):
    n, d = x1.shape
    spt = seq // tm
    n_chunks, _, fc2 = w["w_up"].shape
    fc = fc2 // 2
    d_ff = n_chunks * fc
    hb = tm // HALO
    last = n // HALO - 1
    kern = functools.partial(_ffn_kernel, spt=spt, fc=fc)
    return pl.pallas_call(
        kern,
        grid=(n // tm,),
        in_specs=[pl.BlockSpec((tm, d), lambda i: (i, 0)),
                  pl.BlockSpec((HALO, d), lambda i: (jnp.maximum(i * hb - 1, 0), 0)),
                  pl.BlockSpec((HALO, d), lambda i: (jnp.minimum((i + 1) * hb, last), 0)),
                  _const_spec(w["g_ffn"].shape), _const_spec(w["w_up"].shape),
                  _const_spec(w["conv_w"].shape), _const_spec(w["conv_b"].shape),
                  _const_spec(w["w_down"].shape), _const_spec(w["g_final"].shape)],
        out_specs=pl.BlockSpec((tm, d), lambda i: (i, 0)),
        out_shape=jax.ShapeDtypeStruct((n, d), F32),
        scratch_shapes=[pltpu.VMEM((tm + 2 * HALO, d), BF16), pltpu.VMEM((tm, d_ff), BF16)],
        compiler_params=_params(("parallel",)),
        name="channel_mixer",
    )(x1, x1, x1, w["g_ffn"], w["w_up"], w["conv_w"], w["conv_b"], w["w_down"], w["g_final"])


FFN_CHUNK = 256


def _prep_weights(g_mix, w_in, g_q, w_uq, g_kv, w_ukv, w_attn_out, w_fourier_out, w_out,
                  g_ffn, w_up, conv_w, conv_b, w_down, g_final):
    q_rank = g_q.shape[0]
    kv_rank = g_kv.shape[0]
    d = w_in.shape[0]
    half = QK_ROPE // 2
    o_kr = q_rank + kv_rank
    o_f = o_kr + QK_ROPE
    kr = w_in[:, o_kr:o_f]
    kr_swapped = jnp.concatenate([kr[:, half:], kr[:, :half]], axis=1)
    kr_pad = jnp.zeros((d, LANES - 2 * QK_ROPE), F32)
    w_in_r = jnp.concatenate([w_in[:, :o_kr], kr, kr_swapped, kr_pad, w_in[:, o_f:]], axis=1)

    qh = w_uq.reshape(q_rank, N_HEADS, QK_NOPE + QK_ROPE)
    zq = jnp.zeros((q_rank, N_HEADS, HEAD_PAD - QK_NOPE - QK_ROPE), F32)
    w_q_plain = jnp.concatenate([qh, zq], axis=2)
    w_q_swap = jnp.concatenate([jnp.zeros((q_rank, N_HEADS, QK_NOPE), F32),
                                qh[:, :, QK_NOPE + half:], qh[:, :, QK_NOPE:QK_NOPE + half], zq], axis=2)
    w_q = jnp.concatenate([w_q_plain.reshape(q_rank, -1), w_q_swap.reshape(q_rank, -1)], axis=1)

    kvh = w_ukv.reshape(kv_rank, N_HEADS, QK_NOPE + V_DIM)
    w_k_top = jnp.concatenate([kvh[:, :, :QK_NOPE],
                               jnp.zeros((kv_rank, N_HEADS, HEAD_PAD - QK_NOPE), F32)], axis=2)
    w_v = jnp.concatenate([kvh[:, :, QK_NOPE:],
                           jnp.zeros((kv_rank, N_HEADS, HEAD_PAD - V_DIM), F32)], axis=2)
    eye = jnp.eye(QK_ROPE, dtype=F32)
    place = jnp.concatenate([jnp.zeros((QK_ROPE, QK_NOPE), F32), eye,
                             jnp.zeros((QK_ROPE, HEAD_PAD - QK_NOPE - QK_ROPE), F32)], axis=1)
    place = jnp.tile(place[:, None, :], (1, N_HEADS, 1))
    w_k_bot = jnp.concatenate([place, place,
                               jnp.zeros((LANES - 2 * QK_ROPE, N_HEADS, HEAD_PAD), F32)], axis=0)
    w_k = jnp.concatenate([w_k_top.reshape(kv_rank, -1), w_k_bot.reshape(LANES, -1)], axis=0)

    d_ff = w_down.shape[0]
    nc = d_ff // FFN_CHUNK
    gate_w = w_up[:, :d_ff].reshape(d, nc, FFN_CHUNK)
    val_w = w_up[:, d_ff:].reshape(d, nc, FFN_CHUNK)
    w_up_r = jnp.concatenate([gate_w, val_w], axis=2).transpose(1, 0, 2)
    k3 = conv_w.shape[0]
    cw_r = jnp.concatenate([conv_w[:, :d_ff].reshape(k3, nc, FFN_CHUNK),
                            conv_w[:, d_ff:].reshape(k3, nc, FFN_CHUNK)], axis=2).transpose(1, 0, 2)
    cb_r = jnp.concatenate([conv_b[:d_ff].reshape(nc, 1, FFN_CHUNK),
                            conv_b[d_ff:].reshape(nc, 1, FFN_CHUNK)], axis=2)
    return {
        "g_mix": g_mix[None, :], "w_in": w_in_r.astype(BF16),
        "g_q": g_q[None, :], "w_q": w_q.astype(BF16),
        "g_kv": g_kv[None, :], "w_v": w_v.reshape(kv_rank, -1).astype(BF16), "w_k": w_k.astype(BF16),
        "w_attn_out": w_attn_out.astype(BF16), "w_fourier_out": w_fourier_out.astype(BF16),
        "w_out": w_out.astype(BF16),
        "g_ffn": g_ffn[None, :], "w_up": w_up_r.astype(BF16), "conv_w": cw_r, "conv_b": cb_r,
        "w_down": w_down.astype(BF16), "g_final": g_final[None, :],
    }


def _dft_angles(rows, cols, period):
    prod = (rows.astype(jnp.int32) * cols.astype(jnp.int32)) % period
    return prod.astype(F32) * (2.0 * math.pi / period)


def _split_seq(seq):
    n1 = 1 << ((seq.bit_length() - 1 + 1) // 2)
    assert seq % n1 == 0
    return n1, seq // n1


def _tables(seq):
    half = QK_ROPE // 2
    scale = (QK_NOPE + QK_ROPE) ** -0.5 * math.log2(math.e)
    pos =jnp.arange(seq, dtype=F32)
    inv_freq = ROPE_THETA ** (-jnp.arange(0, QK_ROPE, 2, dtype=F32) / QK_ROPE)
    ang = pos[:, None] * inv_freq[None, :]
    cos, sin = jnp.cos(ang), jnp.sin(ang)
    ones = jnp.ones((seq, QK_NOPE), F32)
    zeros = lambda width: jnp.zeros((seq, width), F32)
    pad = HEAD_PAD - QK_NOPE - QK_ROPE
    ct = scale * jnp.concatenate([ones, cos, cos, zeros(pad)], axis=1)
    st = scale * jnp.concatenate([zeros(QK_NOPE), -sin, sin, zeros(pad)], axis=1)
    ck = jnp.concatenate([cos, cos, -sin, sin, zeros(LANES - 2 * QK_ROPE)], axis=1)

    ch = jnp.arange(F_GROUP_DIM)
    ang_c = _dft_angles(ch[:, None], ch[None, :], F_GROUP_DIM)
    dft_c = jnp.concatenate([jnp.cos(ang_c), -jnp.sin(ang_c)], axis=1).astype(BF16)

    n1, n2 = _split_seq(seq)
    a = jnp.arange(n1)
    ang1 = _dft_angles(a[:, None], a[None, :], n1)
    c1, s1 = jnp.cos(ang1), jnp.sin(ang1)
    m1 = jnp.concatenate([jnp.concatenate([c1, s1], axis=1),
                          jnp.concatenate([-s1, c1], axis=1)], axis=0).astype(BF16)
    sp = (jnp.arange(n1)[:, None] + n1 * jnp.arange(n2)[None, :])
    ang2 = _dft_angles(sp[:, :, None], jnp.arange(n2)[None, None, :], seq)
    gtab = jnp.concatenate([jnp.cos(ang2), jnp.sin(ang2)], axis=2).astype(BF16)
    return {"ct": ct, "st": st, "ck": ck, "dft_c": dft_c, "m1": m1, "gtab": gtab,
            "n1": n1, "n2": n2}


def _pick(seq, pref):
    t = pref
    while seq % t:
        t //= 2
    return t


def _trunk(x, w):
    batch, seq, d = x.shape
    x2d = x.reshape(batch * seq, d)
    tabs = _tables(seq)
    n1, n2 = tabs["n1"], tabs["n2"]
    tm = _pick(seq, 512)
    qt, kp, vt, yr, yi, gates = _inproj(x2d, seq, tabs, w, tm)
    ao = _attention(qt, kp, vt, batch, seq, tm)
    width = yr.shape[1]
    t = _dft1(yr, yi, tabs["m1"], batch, n1, n2)
    norm = 1.0 / math.sqrt(seq * F_GROUP_DIM)
    fo = _dft2(t, tabs["gtab"], batch, n1, n2, width, norm)
    x1 = _merge(x2d, ao, fo, gates, w, tm)
    y = _ffn(x1, seq, w, tm)
    return y.reshape(batch, seq, d)


def kernel(x_prompt, x_sample, g_mix, w_in, g_q, w_uq, g_kv, w_ukv, w_attn_out, w_fourier_out,
           w_out, g_ffn, w_up, conv_w, conv_b, w_down, g_final):
    w = _prep_weights(g_mix[0], w_in[0], g_q[0], w_uq[0], g_kv[0], w_ukv[0], w_attn_out[0],
                      w_fourier_out[0], w_out[0], g_ffn[0], w_up[0], conv_w[0], conv_b[0],
                      w_down[0], g_final)
    return (_trunk(x_prompt, w), _trunk(x_sample, w))
```

```python
import functools
import math

import jax
import jax.numpy as jnp
import numpy as np
from jax import lax
from jax.experimental import pallas as pl
from jax.experimental.pallas import tpu as pltpu

F32 = jnp.float32
BF16 = jnp.bfloat16

N_HEADS = 8
QK_NOPE = 64
QK_ROPE = 32
V_DIM = 64
HEAD_PAD = 128
F_GROUPS = 4
F_GROUP_DIM = 128
ROPE_THETA = 10000.0
EPS = 1e-6

LANES = 128
VMEM_LIMIT = 56 * 1024 * 1024


def _params(sem, vmem=VMEM_LIMIT):
    return pltpu.CompilerParams(dimension_semantics=sem, vmem_limit_bytes=vmem)


def _const_spec(shape):
    nd = len(shape)
    return pl.BlockSpec(shape, lambda *_: (0,) * nd)


def _rms(v, g):
    r = lax.rsqrt(jnp.mean(v * v, axis=-1, keepdims=True) + EPS)
    return v * r * g


def _dot(a, b):
    return jnp.dot(a, b, preferred_element_type=F32)


def _inproj_kernel(x_ref, ct_ref, st_ref, ck_ref, gmix_ref, win_ref, gq_ref, wq_ref,
                   gkv_ref, wv_ref, wk_ref, dft_ref,
                   qt_ref, k_ref, vt_ref, yr_ref, yi_ref, gate_ref, *, q_rank, kv_rank, f_width):
    h = _rms(x_ref[...], gmix_ref[...]).astype(BF16)
    proj = _dot(h, win_ref[...])
    o_kv = q_rank
    o_kr = o_kv + kv_rank
    o_f = o_kr + LANES
    o_g = o_f + f_width
    cqn = _rms(proj[:, :o_kv], gq_ref[...]).astype(BF16)
    ckvn = _rms(proj[:, o_kv:o_kr], gkv_ref[...])

    q2 = _dot(cqn, wq_ref[...])
    ct = ct_ref[...]
    st = st_ref[...]
    hw = N_HEADS * HEAD_PAD
    q = jnp.concatenate(
        [q2[:, lo:lo + HEAD_PAD] * ct + q2[:, hw + lo:hw + lo + HEAD_PAD] * st
         for lo in range(0, hw, HEAD_PAD)], axis=1)
    qt_ref[...] = q.T.astype(BF16)

    lane = lax.broadcasted_iota(jnp.int32, (1, hw), 1)
    ones_col = (lane % HEAD_PAD == V_DIM).astype(F32)
    v = _dot(ckvn.astype(BF16), wv_ref[...]) + ones_col
    vt_ref[...] = v.T.astype(BF16)

    kr = proj[:, o_kr:o_f] * ck_ref[...]
    kin = jnp.concatenate([ckvn, kr], axis=1).astype(BF16)
    k_ref[...] = _dot(kin, wk_ref[...]).astype(BF16)

    xf = proj[:, o_f:o_g].astype(BF16)
    for g in range(F_GROUPS):
        lo = g * F_GROUP_DIM
        y = _dot(xf[:, lo:lo + F_GROUP_DIM], dft_ref[...])
        yr_ref[:, lo:lo + F_GROUP_DIM] = y[:, :F_GROUP_DIM].astype(BF16)
        yi_ref[:, lo:lo + F_GROUP_DIM] = y[:, F_GROUP_DIM:].astype(BF16)

    gate_ref[...] = jax.nn.sigmoid(proj[:, o_g:]).astype(BF16)


def _inproj(x2d, seq, tabs, w, tm):
    n, d = x2d.shape
    spt = seq // tm
    q_rank = w["g_q"].shape[1]
    kv_rank = w["g_kv"].shape[1]
    f_width = F_GROUPS * F_GROUP_DIM
    hw = N_HEADS * HEAD_PAD
    n_gate = w["w_in"].shape[1] - (q_rank + kv_rank + LANES + f_width)
    row = lambda width: pl.BlockSpec((tm, width), lambda i: (i, 0))
    tab = pl.BlockSpec((tm, LANES), lambda i: (i % spt, 0))
    tblk = pl.BlockSpec((None, None, hw, tm), lambda i: (i // spt, i % spt, 0, 0))
    t_shape = jax.ShapeDtypeStruct((n // seq, spt, hw, tm), BF16)
    kern = functools.partial(_inproj_kernel, q_rank=q_rank, kv_rank=kv_rank, f_width=f_width)
    return pl.pallas_call(
        kern,
        grid=(n // tm,),
        in_specs=[row(d), tab, tab, tab,
                  _const_spec(w["g_mix"].shape), _const_spec(w["w_in"].shape),
                  _const_spec(w["g_q"].shape), _const_spec(w["w_q"].shape),
                  _const_spec(w["g_kv"].shape), _const_spec(w["w_v"].shape),
                  _const_spec(w["w_k"].shape), _const_spec(tabs["dft_c"].shape)],
        out_specs=[tblk, row(hw), tblk, row(f_width), row(f_width), row(n_gate)],
        out_shape=[t_shape, jax.ShapeDtypeStruct((n, hw), BF16), t_shape,
                   jax.ShapeDtypeStruct((n, f_width), BF16),
                   jax.ShapeDtypeStruct((n, f_width), BF16), jax.ShapeDtypeStruct((n, n_gate), BF16)],
        compiler_params=_params(("parallel",)),
        name="inproj",
    )(x2d, tabs["ct"], tabs["st"], tabs["ck"], w["g_mix"], w["w_in"], w["g_q"], w["w_q"],
      w["g_kv"], w["w_v"], w["w_k"], tabs["dft_c"])


BF16_SUBLANES = 16
PV_ROWS = HEAD_PAD


def _attn_unroll(n_kv):
    return 8 if n_kv > 8 else 4


def _attn_kernel(qt_ref, k_ref, vt_ref, o_ref, m_ref, acc_ref, s_ref, smax_ref, *, n_kv):
    m_ref[...] = jnp.full(m_ref.shape, -jnp.inf, F32)
    acc_ref[...] = jnp.zeros(acc_ref.shape, F32)

    def scores(j, buf):
        for hh in range(2):
            lo = hh * HEAD_PAD
            s = _dot(k_ref[j, :, lo:lo + HEAD_PAD], qt_ref[lo:lo + HEAD_PAD, :])
            s_ref[buf, hh] = s
            smax_ref[buf, hh] = jnp.max(s, axis=0, keepdims=True)

    def update(j, buf):
        for hh in range(2):
            lo = hh * HEAD_PAD
            m_prev = m_ref[hh]
            m_new = jnp.maximum(m_prev, smax_ref[buf, hh])
            p = jnp.exp2(s_ref[buf, hh] - m_new).astype(BF16)
            pv = _dot(vt_ref[j, lo:lo + PV_ROWS, :], p)
            acc_ref[hh] = jnp.exp2(m_prev - m_new) * acc_ref[hh] + pv
            m_ref[hh] = m_new

    unroll = s_ref.shape[0]

    def run(first, count, last):
        for u in range(count):
            if not (last and u == count - 1):
                scores(first + u + 1, (u + 1) % unroll)
            update(first + u, u)

    def body(jj, carry):
        run(unroll * jj, unroll, False)
        return carry

    trips = (n_kv - 1) // unroll
    scores(0, 0)
    lax.fori_loop(0, jnp.minimum(pl.program_id(2) + trips, trips), body, 0)
    run(unroll * trips, n_kv - unroll * trips, True)
    o_t = jnp.concatenate(
        [acc_ref[hh, :V_DIM, :] / acc_ref[hh, V_DIM:V_DIM + 1, :] for hh in range(2)], axis=0)
    o_ref[...] = o_t.T.astype(o_ref.dtype)


def _attention(qt, kp, vt, batch, seq, tile):
    n, hw = kp.shape
    nblk = seq // tile
    pair = 2 * HEAD_PAD
    k4 = kp.reshape(batch, nblk, tile, hw)
    kern = functools.partial(_attn_kernel, n_kv=nblk)
    return pl.pallas_call(
        kern,
        grid=(batch, N_HEADS // 2, nblk),
        in_specs=[pl.BlockSpec((None, None, pair, tile), lambda b, hp, i: (b, i, hp, 0)),
                  pl.BlockSpec((None, nblk, tile, pair), lambda b, hp, i: (b, 0, 0, hp)),
                  pl.BlockSpec((None, nblk, pair, tile), lambda b, hp, i: (b, 0, hp, 0))],
        out_specs=pl.BlockSpec((tile, HEAD_PAD), lambda b, hp, i: (b * nblk + i, hp)),
        out_shape=jax.ShapeDtypeStruct((n, N_HEADS * V_DIM), BF16),
        scratch_shapes=[pltpu.VMEM((2, 1, tile), F32), pltpu.VMEM((2, PV_ROWS, tile), F32),
                        pltpu.VMEM((_attn_unroll(nblk), 2, tile, tile), F32),
                        pltpu.VMEM((_attn_unroll(nblk), 2, 1, tile), F32)],
        compiler_params=_params(("parallel", "parallel", "arbitrary")),
        name="attention",
    )(qt, k4, vt)


DFT_ROWS = BF16_SUBLANES


def _dft1_kernel(yr_ref, yi_ref, m1_ref, t_ref, scr_ref):
    scr_ref[0] = yr_ref[...].astype(F32)
    scr_ref[1] = yi_ref[...].astype(F32)
    for b in range(DFT_ROWS):
        y = jnp.concatenate([scr_ref[0, :, b, :], scr_ref[1, :, b, :]], axis=0).astype(BF16)
        t_ref[b] = _dot(m1_ref[...], y).astype(BF16)


def _dft1(yr, yi, m1, batch, n1, n2):
    width = yr.shape[1]
    yr4 = yr.reshape(batch, n1, n2, width)
    yi4 = yi.reshape(batch, n1, n2, width)
    blk = pl.BlockSpec((None, n1, DFT_ROWS, width), lambda bt, j: (bt, 0, j, 0))
    return pl.pallas_call(
        _dft1_kernel,
        grid=(batch, n2 // DFT_ROWS),
        in_specs=[blk, blk, _const_spec(m1.shape)],
        out_specs=pl.BlockSpec((None, DFT_ROWS, 2 * n1, width), lambda bt, j: (bt, j, 0, 0)),
        out_shape=jax.ShapeDtypeStruct((batch, n2, 2 * n1, width), BF16),
        scratch_shapes=[pltpu.VMEM((2, n1, DFT_ROWS, width), F32)],
        compiler_params=_params(("parallel", "parallel")),
        name="seq_dft_stage1",
    )(yr4, yi4, m1)


def _dft2_kernel(tr_ref, ti_ref, g_ref, o_ref, scr_ref, oscr_ref, *, norm):
    scr_ref[0] = tr_ref[...].astype(F32)
    scr_ref[1] = ti_ref[...].astype(F32)
    for j in range(DFT_ROWS):
        t = jnp.concatenate([scr_ref[0, :, j, :], scr_ref[1, :, j, :]], axis=0).astype(BF16)
        oscr_ref[:, j, :] = _dot(g_ref[j], t) * norm
    o_ref[...] = oscr_ref[...].astype(o_ref.dtype)


def _dft2(t, gtab, batch, n1, n2, width, norm):
    cblks = n1 // DFT_ROWS
    kern = functools.partial(_dft2_kernel, norm=norm)
    out = pl.pallas_call(
        kern,
        grid=(batch, cblks),
        in_specs=[pl.BlockSpec((None, n2, DFT_ROWS, width), lambda bt, c: (bt, 0, c, 0)),
                  pl.BlockSpec((None, n2, DFT_ROWS, width), lambda bt, c: (bt, 0, cblks + c, 0)),
                  pl.BlockSpec((DFT_ROWS, n2, 2 * n2), lambda bt, c: (c, 0, 0))],
        out_specs=pl.BlockSpec((None, n2, DFT_ROWS, width), lambda bt, c: (bt, 0, c, 0)),
        out_shape=jax.ShapeDtypeStruct((batch, n2, n1, width), BF16),
        scratch_shapes=[pltpu.VMEM((2, n2, DFT_ROWS, width), F32),
                        pltpu.VMEM((n2, DFT_ROWS, width), F32)],
        compiler_params=_params(("parallel", "parallel")),
        name="seq_dft_stage2",
    )(t, t, gtab)
    return out.reshape(batch * n1 * n2, width)


def _merge_kernel(x_ref, ao_ref, f_ref, g_ref, wa_ref, wf_ref, wo_ref, o_ref):
    d = x_ref.shape[1]
    a = _dot(ao_ref[...], wa_ref[...])
    f = _dot(f_ref[...], wf_ref[...])
    g = g_ref[...].astype(F32)
    mix = (g[:, :d] * a + g[:, d:] * f).astype(BF16)
    o_ref[...] = x_ref[...] + _dot(mix, wo_ref[...])


def _merge(x2d, ao, fo, gates, w, tm):
    n, d = x2d.shape
    row = lambda width: pl.BlockSpec((tm, width), lambda i: (i, 0))
    return pl.pallas_call(
        _merge_kernel,
        grid=(n // tm,),
        in_specs=[row(d), row(ao.shape[1]), row(fo.shape[1]), row(gates.shape[1]),
                  _const_spec(w["w_attn_out"].shape), _const_spec(w["w_fourier_out"].shape),
                  _const_spec(w["w_out"].shape)],
        out_specs=row(d),
        out_shape=jax.ShapeDtypeStruct((n, d), F32),
        compiler_params=_params(("parallel",)),
        name="merge",
    )(x2d, ao, fo, gates, w["w_attn_out"], w["w_fourier_out"], w["w_out"])


HALO = 8


def _ffn_kernel(x_ref, xp_ref, xn_ref, gffn_ref, wup_ref, cw_ref, cb_ref, wdn_ref, gfin_ref,
                o_ref, hext_ref, act_ref, *, spt, fc):
    tm = x_ref.shape[0]
    n_chunks = wup_ref.shape[0]
    i = pl.program_id(0)
    has_prev = (i % spt != 0).astype(F32)
    has_next = (i % spt != spt - 1).astype(F32)
    g = gffn_ref[...]
    x = x_ref[...]
    hext_ref[0:HALO, :] = (_rms(xp_ref[...], g) * has_prev).astype(BF16)
    hext_ref[HALO:HALO + tm, :] = _rms(x, g).astype(BF16)
    hext_ref[HALO + tm:, :] = (_rms(xn_ref[...], g) * has_next).astype(BF16)
    hext = hext_ref[...]
    rows = tm + 2 * HALO
    for c in range(n_chunks):
        u = _dot(hext, wup_ref[c])
        cw = cw_ref[c]
        conv = (pltpu.roll(u, 1, axis=0) * cw[0:1] + u * cw[1:2]
                + pltpu.roll(u, rows - 1, axis=0) * cw[2:3] + cb_ref[c])
        conv = conv[HALO:HALO + tm]
        gate = conv[:, :fc]
        act_ref[:, c * fc:(c + 1) * fc] = (gate * jax.nn.sigmoid(gate) * conv[:, fc:]).astype(BF16)
    x2 = x + _dot(act_ref[...], wdn_ref[...])
    o_ref[...] = _rms(x2, gfin_ref[...])


def _ffn(x1, seq, w, tm):
    n, d = x1.shape
    spt = seq // tm
    n_chunks, _, fc2 = w["w_up"].shape
    fc = fc2 // 2
    d_ff = n_chunks * fc
    hb = tm // HALO
    last = n // HALO - 1
    kern = functools.partial(_ffn_kernel, spt=spt, fc=fc)
    return pl.pallas_call(
        kern,
        grid=(n // tm,),
        in_specs=[pl.BlockSpec((tm, d), lambda i: (i, 0)),
                  pl.BlockSpec((HALO, d), lambda i: (jnp.maximum(i * hb - 1, 0), 0)),
                  pl.BlockSpec((HALO, d), lambda i: (jnp.minimum((i + 1) * hb, last), 0)),
                  _const_spec(w["g_ffn"].shape), _const_spec(w["w_up"].shape),
                  _const_spec(w["conv_w"].shape), _const_spec(w["conv_b"].shape),
                  _const_spec(w["w_down"].shape), _const_spec(w["g_final"].shape)],
        out_specs=pl.BlockSpec((tm, d), lambda i: (i, 0)),
        out_shape=jax.ShapeDtypeStruct((n, d), F32),
        scratch_shapes=[pltpu.VMEM((tm + 2 * HALO, d), BF16), pltpu.VMEM((tm, d_ff), BF16)],
        compiler_params=_params(("parallel",)),
        name="channel_mixer",
    )(x1, x1, x1, w["g_ffn"], w["w_up"], w["conv_w"], w["conv_b"], w["w_down"], w["g_final"])


FFN_CHUNK = 256


def _prep_weights(g_mix, w_in, g_q, w_uq, g_kv, w_ukv, w_attn_out, w_fourier_out, w_out,
                  g_ffn, w_up, conv_w, conv_b, w_down, g_final):
    q_rank = g_q.shape[0]
    kv_rank = g_kv.shape[0]
    d = w_in.shape[0]
    half = QK_ROPE // 2
    o_kr = q_rank + kv_rank
    o_f = o_kr + QK_ROPE
    kr = w_in[:, o_kr:o_f]
    kr_swapped = jnp.concatenate([kr[:, half:], kr[:, :half]], axis=1)
    kr_pad = jnp.zeros((d, LANES - 2 * QK_ROPE), F32)
    w_in_r = jnp.concatenate([w_in[:, :o_kr], kr, kr_swapped, kr_pad, w_in[:, o_f:]], axis=1)

    qh = w_uq.reshape(q_rank, N_HEADS, QK_NOPE + QK_ROPE)
    zq = jnp.zeros((q_rank, N_HEADS, HEAD_PAD - QK_NOPE - QK_ROPE), F32)
    w_q_plain = jnp.concatenate([qh, zq], axis=2)
    w_q_swap = jnp.concatenate([jnp.zeros((q_rank, N_HEADS, QK_NOPE), F32),
                                qh[:, :, QK_NOPE + half:], qh[:, :, QK_NOPE:QK_NOPE + half], zq], axis=2)
    w_q = jnp.concatenate([w_q_plain.reshape(q_rank, -1), w_q_swap.reshape(q_rank, -1)], axis=1)

    kvh = w_ukv.reshape(kv_rank, N_HEADS, QK_NOPE + V_DIM)
    w_k_top = jnp.concatenate([kvh[:, :, :QK_NOPE],
                               jnp.zeros((kv_rank, N_HEADS, HEAD_PAD - QK_NOPE), F32)], axis=2)
    w_v = jnp.concatenate([kvh[:, :, QK_NOPE:],
                           jnp.zeros((kv_rank, N_HEADS, HEAD_PAD - V_DIM), F32)], axis=2)
    eye = jnp.eye(QK_ROPE, dtype=F32)
    place = jnp.concatenate([jnp.zeros((QK_ROPE, QK_NOPE), F32), eye,
                             jnp.zeros((QK_ROPE, HEAD_PAD - QK_NOPE - QK_ROPE), F32)], axis=1)
    place = jnp.tile(place[:, None, :], (1, N_HEADS, 1))
    w_k_bot = jnp.concatenate([place, place,
                               jnp.zeros((LANES - 2 * QK_ROPE, N_HEADS, HEAD_PAD), F32)], axis=0)
    w_k = jnp.concatenate([w_k_top.reshape(kv_rank, -1), w_k_bot.reshape(LANES, -1)], axis=0)

    d_ff = w_down.shape[0]
    nc = d_ff // FFN_CHUNK
    gate_w = w_up[:, :d_ff].reshape(d, nc, FFN_CHUNK)
    val_w = w_up[:, d_ff:].reshape(d, nc, FFN_CHUNK)
    w_up_r = jnp.concatenate([gate_w, val_w], axis=2).transpose(1, 0, 2)
    k3 = conv_w.shape[0]
    cw_r = jnp.concatenate([conv_w[:, :d_ff].reshape(k3, nc, FFN_CHUNK),
                            conv_w[:, d_ff:].reshape(k3, nc, FFN_CHUNK)], axis=2).transpose(1, 0, 2)
    cb_r = jnp.concatenate([conv_b[:d_ff].reshape(nc, 1, FFN_CHUNK),
                            conv_b[d_ff:].reshape(nc, 1, FFN_CHUNK)], axis=2)
    return {
        "g_mix": g_mix[None, :], "w_in": w_in_r.astype(BF16),
        "g_q": g_q[None, :], "w_q": w_q.astype(BF16),
        "g_kv": g_kv[None, :], "w_v": w_v.reshape(kv_rank, -1).astype(BF16), "w_k": w_k.astype(BF16),
        "w_attn_out": w_attn_out.astype(BF16), "w_fourier_out": w_fourier_out.astype(BF16),
        "w_out": w_out.astype(BF16),
        "g_ffn": g_ffn[None, :], "w_up": w_up_r.astype(BF16), "conv_w": cw_r, "conv_b": cb_r,
        "w_down": w_down.astype(BF16), "g_final": g_final[None, :],
    }


def _dft_angles(rows, cols, period):
    prod = (rows.astype(jnp.int32) * cols.astype(jnp.int32)) % period
    return prod.astype(F32) * (2.0 * math.pi / period)


def _split_seq(seq):
    n1 = 1 << ((seq.bit_length() - 1 + 1) // 2)
    assert seq % n1 == 0
    return n1, seq // n1


def _tables(seq):
    half = QK_ROPE // 2
    scale = (QK_NOPE + QK_ROPE) ** -0.5 * math.log2(math.e)
    pos =jnp.arange(seq, dtype=F32)
    inv_freq = ROPE_THETA ** (-jnp.arange(0, QK_ROPE, 2, dtype=F32) / QK_ROPE)
    ang = pos[:, None] * inv_freq[None, :]
    cos, sin = jnp.cos(ang), jnp.sin(ang)
    ones = jnp.ones((seq, QK_NOPE), F32)
    zeros = lambda width: jnp.zeros((seq, width), F32)
    pad = HEAD_PAD - QK_NOPE - QK_ROPE
    ct = scale * jnp.concatenate([ones, cos, cos, zeros(pad)], axis=1)
    st = scale * jnp.concatenate([zeros(QK_NOPE), -sin, sin, zeros(pad)], axis=1)
    ck = jnp.concatenate([cos, cos, -sin, sin, zeros(LANES - 2 * QK_ROPE)], axis=1)

    ch = jnp.arange(F_GROUP_DIM)
    ang_c = _dft_angles(ch[:, None], ch[None, :], F_GROUP_DIM)
    dft_c = jnp.concatenate([jnp.cos(ang_c), -jnp.sin(ang_c)], axis=1).astype(BF16)

    n1, n2 = _split_seq(seq)
    a = jnp.arange(n1)
    ang1 = _dft_angles(a[:, None], a[None, :], n1)
    c1, s1 = jnp.cos(ang1), jnp.sin(ang1)
    m1 = jnp.concatenate([jnp.concatenate([c1, s1], axis=1),
                          jnp.concatenate([-s1, c1], axis=1)], axis=0).astype(BF16)
    sp = (jnp.arange(n1)[:, None] + n1 * jnp.arange(n2)[None, :])
    ang2 = _dft_angles(sp[:, :, None], jnp.arange(n2)[None, None, :], seq)
    gtab = jnp.concatenate([jnp.cos(ang2), jnp.sin(ang2)], axis=2).astype(BF16)
    return {"ct": ct, "st": st, "ck": ck, "dft_c": dft_c, "m1": m1, "gtab": gtab,
            "n1": n1, "n2": n2}


def _pick(seq, pref):
    t = pref
    while seq % t:
        t //= 2
    return t


def _trunk(x, w):
    batch, seq, d = x.shape
    x2d = x.reshape(batch * seq, d)
    tabs = _tables(seq)
    n1, n2 = tabs["n1"], tabs["n2"]
    tm = _pick(seq, 512)
    qt, kp, vt, yr, yi, gates = _inproj(x2d, seq, tabs, w, tm)
    ao = _attention(qt, kp, vt, batch, seq, tm)
    width = yr.shape[1]
    t = _dft1(yr, yi, tabs["m1"], batch, n1, n2)
    norm = 1.0 / math.sqrt(seq * F_GROUP_DIM)
    fo = _dft2(t, tabs["gtab"], batch, n1, n2, width, norm)
    x1 = _merge(x2d, ao, fo, gates, w, tm)
    y = _ffn(x1, seq, w, tm)
    return y.reshape(batch, seq, d)


def kernel(x_prompt, x_sample, g_mix, w_in, g_q, w_uq, g_kv, w_ukv, w_attn_out, w_fourier_out,
           w_out, g_ffn, w_up, conv_w, conv_b, w_down, g_final):
    w = _prep_weights(g_mix[0], w_in[0], g_q[0], w_uq[0], g_kv[0], w_ukv[0], w_attn_out[0],
                      w_fourier_out[0], w_out[0], g_ffn[0], w_up[0], conv_w[0], conv_b[0],
                      w_down[0], g_final)
    return (_trunk(x_prompt, w), _trunk(x_sample, w))
```

```python
import functools
import math

import jax
import jax.numpy as jnp
import numpy as np
from jax import lax
from jax.experimental import pallas as pl
from jax.experimental.pallas import tpu as pltpu

F32 = jnp.float32
BF16 = jnp.bfloat16

N_HEADS = 8
QK_NOPE = 64
QK_ROPE = 32
V_DIM = 64
HEAD_PAD = 128
F_GROUPS = 4
F_GROUP_DIM = 128
ROPE_THETA = 10000.0
EPS = 1e-6

LANES = 128
VMEM_LIMIT = 56 * 1024 * 1024


def _params(sem, vmem=VMEM_LIMIT):
    return pltpu.CompilerParams(dimension_semantics=sem, vmem_limit_bytes=vmem)


def _const_spec(shape):
    nd = len(shape)
    return pl.BlockSpec(shape, lambda *_: (0,) * nd, pipeline_mode=pl.Buffered(1))


def _rms(v, g):
    r = lax.rsqrt(jnp.mean(v * v, axis=-1, keepdims=True) + EPS)
    return v * r * g


def _dot(a, b):
    return jnp.dot(a, b, preferred_element_type=F32)


def _inproj_kernel(x_ref, ct_ref, st_ref, ck_ref, gmix_ref, win_ref, gq_ref, wq_ref,
                   gkv_ref, wv_ref, wk_ref, dft_ref,
                   qt_ref, k_ref, vt_ref, yr_ref, yi_ref, gate_ref, *, q_rank, kv_rank, f_width):
    h = _rms(x_ref[...], gmix_ref[...]).astype(BF16)
    proj = _dot(h, win_ref[...])
    o_kv = q_rank
    o_kr = o_kv + kv_rank
    o_f = o_kr + LANES
    o_g = o_f + f_width
    cqn = _rms(proj[:, :o_kv], gq_ref[...]).astype(BF16)
    ckvn = _rms(proj[:, o_kv:o_kr], gkv_ref[...])

    q2 = _dot(cqn, wq_ref[...])
    ct = ct_ref[...]
    st = st_ref[...]
    hw = N_HEADS * HEAD_PAD
    q = jnp.concatenate(
        [q2[:, lo:lo + HEAD_PAD] * ct + q2[:, hw + lo:hw + lo + HEAD_PAD] * st
         for lo in range(0, hw, HEAD_PAD)], axis=1)
    qt_ref[...] = q.T.astype(BF16)

    lane = lax.broadcasted_iota(jnp.int32, (1, hw), 1)
    ones_col = (lane % HEAD_PAD == V_DIM).astype(F32)
    v = _dot(ckvn.astype(BF16), wv_ref[...]) + ones_col
    vt_ref[...] = v.T.astype(BF16)

    kr = proj[:, o_kr:o_f] * ck_ref[...]
    kin = jnp.concatenate([ckvn, kr], axis=1).astype(BF16)
    k_ref[...] = _dot(kin, wk_ref[...]).astype(BF16)

    xf = proj[:, o_f:o_g].astype(BF16)
    for g in range(F_GROUPS):
        lo = g * F_GROUP_DIM
        y = _dot(xf[:, lo:lo + F_GROUP_DIM], dft_ref[...])
        yr_ref[:, lo:lo + F_GROUP_DIM] = y[:, :F_GROUP_DIM].astype(BF16)
        yi_ref[:, lo:lo + F_GROUP_DIM] = y[:, F_GROUP_DIM:].astype(BF16)

    gate_ref[...] = jax.nn.sigmoid(proj[:, o_g:]).astype(BF16)


def _inproj(x2d, seq, tabs, w, tm):
    n, d = x2d.shape
    spt = seq // tm
    q_rank = w["g_q"].shape[1]
    kv_rank = w["g_kv"].shape[1]
    f_width = F_GROUPS * F_GROUP_DIM
    hw = N_HEADS * HEAD_PAD
    n_gate = w["w_in"].shape[1] - (q_rank + kv_rank + LANES + f_width)
    row = lambda width: pl.BlockSpec((tm, width), lambda i: (i, 0))
    tab = pl.BlockSpec((tm, LANES), lambda i: (i % spt, 0))
    tblk = pl.BlockSpec((None, None, hw, tm), lambda i: (i // spt, i % spt, 0, 0))
    t_shape = jax.ShapeDtypeStruct((n // seq, spt, hw, tm), BF16)
    kern = functools.partial(_inproj_kernel, q_rank=q_rank, kv_rank=kv_rank, f_width=f_width)
    return pl.pallas_call(
        kern,
        grid=(n // tm,),
        in_specs=[row(d), tab, tab, tab,
                  _const_spec(w["g_mix"].shape), _const_spec(w["w_in"].shape),
                  _const_spec(w["g_q"].shape), _const_spec(w["w_q"].shape),
                  _const_spec(w["g_kv"].shape), _const_spec(w["w_v"].shape),
                  _const_spec(w["w_k"].shape), _const_spec(tabs["dft_c"].shape)],
        out_specs=[tblk, row(hw), tblk, row(f_width), row(f_width), row(n_gate)],
        out_shape=[t_shape, jax.ShapeDtypeStruct((n, hw), BF16), t_shape,
                   jax.ShapeDtypeStruct((n, f_width), BF16),
                   jax.ShapeDtypeStruct((n, f_width), BF16), jax.ShapeDtypeStruct((n, n_gate), BF16)],
        compiler_params=_params(("parallel",)),
        name="inproj",
    )(x2d, tabs["ct"], tabs["st"], tabs["ck"], w["g_mix"], w["w_in"], w["g_q"], w["w_q"],
      w["g_kv"], w["w_v"], w["w_k"], tabs["dft_c"])


BF16_SUBLANES = 16
PV_ROWS = HEAD_PAD


def _attn_unroll(n_kv):
    return 8 if n_kv > 8 else 4


ATTN_Q_TILES = 4


def _attn_kernel(qt_ref, k_ref, vt_ref, o_ref, m_ref, acc_ref, s_ref, smax_ref, *, n_kv):
    q_tiles, _, tile = qt_ref.shape
    unroll = s_ref.shape[0]
    trips = (n_kv - 1) // unroll

    def one_query_tile(t, carry):
        m_ref[...] = jnp.full(m_ref.shape, -jnp.inf, F32)
        acc_ref[...] = jnp.zeros(acc_ref.shape, F32)

        def scores(j, buf):
            for hh in range(2):
                lo = hh * HEAD_PAD
                s = _dot(k_ref[j, :, lo:lo + HEAD_PAD], qt_ref[t, lo:lo + HEAD_PAD, :])
                s_ref[buf, hh] = s
                smax_ref[buf, hh] = jnp.max(s, axis=0, keepdims=True)

        def update(j, buf):
            for hh in range(2):
                lo = hh * HEAD_PAD
                m_prev = m_ref[hh]
                m_new = jnp.maximum(m_prev, smax_ref[buf, hh])
                p = jnp.exp2(s_ref[buf, hh] - m_new).astype(BF16)
                pv = _dot(vt_ref[j, lo:lo + PV_ROWS, :], p)
                acc_ref[hh] = jnp.exp2(m_prev - m_new) * acc_ref[hh] + pv
                m_ref[hh] = m_new

        def run(first, count, last):
            for u in range(count):
                if not (last and u == count - 1):
                    scores(first + u + 1, (u + 1) % unroll)
                update(first + u, u)

        def body(jj, inner):
            run(unroll * jj, unroll, False)
            return inner

        scores(0, 0)
        lax.fori_loop(0, jnp.minimum(t + trips, trips), body, 0)
        run(unroll * trips, n_kv - unroll * trips, True)
        o_t = jnp.concatenate(
            [acc_ref[hh, :V_DIM, :] / acc_ref[hh, V_DIM:V_DIM + 1, :] for hh in range(2)], axis=0)
        o_ref[pl.ds(pl.multiple_of(t * tile, tile), tile), :] = o_t.T.astype(o_ref.dtype)
        return carry

    lax.fori_loop(0, q_tiles, one_query_tile, 0)


def _attention(qt, kp, vt, batch, seq, tile):
    n, hw = kp.shape
    nblk = seq // tile
    pair = 2 * HEAD_PAD
    k4 = kp.reshape(batch, nblk, tile, hw)
    q_tiles = math.gcd(nblk, ATTN_Q_TILES)
    nq = nblk // q_tiles
    kern = functools.partial(_attn_kernel, n_kv=nblk)
    return pl.pallas_call(
        kern,
        grid=(batch, N_HEADS // 2, nq),
        in_specs=[pl.BlockSpec((None, q_tiles, pair, tile), lambda b, hp, i: (b, i, hp, 0)),
                  pl.BlockSpec((None, nblk, tile, pair), lambda b, hp, i: (b, 0, 0, hp)),
                  pl.BlockSpec((None, nblk, pair, tile), lambda b, hp, i: (b, 0, hp, 0))],
        out_specs=pl.BlockSpec((q_tiles * tile, HEAD_PAD), lambda b, hp, i: (b * nq + i, hp)),
        out_shape=jax.ShapeDtypeStruct((n, N_HEADS * V_DIM), BF16),
        scratch_shapes=[pltpu.VMEM((2, 1, tile), F32), pltpu.VMEM((2, PV_ROWS, tile), F32),
                        pltpu.VMEM((_attn_unroll(nblk), 2, tile, tile), F32),
                        pltpu.VMEM((_attn_unroll(nblk), 2, 1, tile), F32)],
        compiler_params=_params(("parallel", "parallel", "arbitrary")),
        name="attention",
    )(qt, k4, vt)


DFT_ROWS = BF16_SUBLANES


def _dft1_kernel(yr_ref, yi_ref, m1_ref, t_ref, scr_ref):
    scr_ref[0] = yr_ref[...].astype(F32)
    scr_ref[1] = yi_ref[...].astype(F32)
    for b in range(DFT_ROWS):
        y = jnp.concatenate([scr_ref[0, :, b, :], scr_ref[1, :, b, :]], axis=0).astype(BF16)
        t_ref[b] = _dot(m1_ref[...], y).astype(BF16)


def _dft1(yr, yi, m1, batch, n1, n2):
    width = yr.shape[1]
    yr4 = yr.reshape(batch, n1, n2, width)
    yi4 = yi.reshape(batch, n1, n2, width)
    blk = pl.BlockSpec((None, n1, DFT_ROWS, width), lambda bt, j: (bt, 0, j, 0))
    return pl.pallas_call(
        _dft1_kernel,
        grid=(batch, n2 // DFT_ROWS),
        in_specs=[blk, blk, _const_spec(m1.shape)],
        out_specs=pl.BlockSpec((None, DFT_ROWS, 2 * n1, width), lambda bt, j: (bt, j, 0, 0)),
        out_shape=jax.ShapeDtypeStruct((batch, n2, 2 * n1, width), BF16),
        scratch_shapes=[pltpu.VMEM((2, n1, DFT_ROWS, width), F32)],
        compiler_params=_params(("parallel", "parallel")),
        name="seq_dft_stage1",
    )(yr4, yi4, m1)


def _dft2_kernel(tr_ref, ti_ref, g_ref, o_ref, scr_ref, oscr_ref, *, norm):
    scr_ref[0] = tr_ref[...].astype(F32)
    scr_ref[1] = ti_ref[...].astype(F32)
    for j in range(DFT_ROWS):
        t = jnp.concatenate([scr_ref[0, :, j, :], scr_ref[1, :, j, :]], axis=0).astype(BF16)
        oscr_ref[:, j, :] = _dot(g_ref[j], t) * norm
    o_ref[...] = oscr_ref[...].astype(o_ref.dtype)


def _dft2(t, gtab, batch, n1, n2, width, norm):
    cblks = n1 // DFT_ROWS
    kern = functools.partial(_dft2_kernel, norm=norm)
    out = pl.pallas_call(
        kern,
        grid=(batch, cblks),
        in_specs=[pl.BlockSpec((None, n2, DFT_ROWS, width), lambda bt, c: (bt, 0, c, 0)),
                  pl.BlockSpec((None, n2, DFT_ROWS, width), lambda bt, c: (bt, 0, cblks + c, 0)),
                  pl.BlockSpec((DFT_ROWS, n2, 2 * n2), lambda bt, c: (c, 0, 0))],
        out_specs=pl.BlockSpec((None, n2, DFT_ROWS, width), lambda bt, c: (bt, 0, c, 0)),
        out_shape=jax.ShapeDtypeStruct((batch, n2, n1, width), BF16),
        scratch_shapes=[pltpu.VMEM((2, n2, DFT_ROWS, width), F32),
                        pltpu.VMEM((n2, DFT_ROWS, width), F32)],
        compiler_params=_params(("parallel", "parallel")),
        name="seq_dft_stage2",
    )(t, t, gtab)
    return out.reshape(batch * n1 * n2, width)


def _merge_kernel(x_ref, ao_ref, f_ref, g_ref, wa_ref, wf_ref, wo_ref, o_ref):
    d = x_ref.shape[1]
    a = _dot(ao_ref[...], wa_ref[...])
    f = _dot(f_ref[...], wf_ref[...])
    g = g_ref[...].astype(F32)
    mix = (g[:, :d] * a + g[:, d:] * f).astype(BF16)
    o_ref[...] = x_ref[...] + _dot(mix, wo_ref[...])


def _merge(x2d, ao, fo, gates, w, tm):
    n, d = x2d.shape
    row = lambda width: pl.BlockSpec((tm, width), lambda i: (i, 0))
    return pl.pallas_call(
        _merge_kernel,
        grid=(n // tm,),
        in_specs=[row(d), row(ao.shape[1]), row(fo.shape[1]), row(gates.shape[1]),
                  _const_spec(w["w_attn_out"].shape), _const_spec(w["w_fourier_out"].shape),
                  _const_spec(w["w_out"].shape)],
        out_specs=row(d),
        out_shape=jax.ShapeDtypeStruct((n, d), F32),
        compiler_params=_params(("parallel",)),
        name="merge",
    )(x2d, ao, fo, gates, w["w_attn_out"], w["w_fourier_out"], w["w_out"])


HALO = 8


def _ffn_kernel(x_ref, xp_ref, xn_ref, gffn_ref, wup_ref, cw_ref, cb_ref, wdn_ref, gfin_ref,
                o_ref, hext_ref, act_ref, *, spt, fc):
    tm = x_ref.shape[0]
    n_chunks = wup_ref.shape[0]
    i = pl.program_id(0)
    has_prev = (i % spt != 0).astype(F32)
    has_next = (i % spt != spt - 1).astype(F32)
    g = gffn_ref[...]
    x = x_ref[...]
    hext_ref[0:HALO, :] = (_rms(xp_ref[...], g) * has_prev).astype(BF16)
    hext_ref[HALO:HALO + tm, :] = _rms(x, g).astype(BF16)
    hext_ref[HALO + tm:, :] = (_rms(xn_ref[...], g) * has_next).astype(BF16)
    hext = hext_ref[...]
    rows = tm + 2 * HALO
    for c in range(n_chunks):
        u = _dot(hext, wup_ref[c])
        cw = cw_ref[c]
        conv = (pltpu.roll(u, 1, axis=0) * cw[0:1] + u * cw[1:2]
                + pltpu.roll(u, rows - 1, axis=0) * cw[2:3] + cb_ref[c])
        conv = conv[HALO:HALO + tm]
        gate = conv[:, :fc]
        act_ref[:, c * fc:(c + 1) * fc] = (gate * jax.nn.sigmoid(gate) * conv[:, fc:]).astype(BF16)
    x2 = x + _dot(act_ref[...], wdn_ref[...])
    o_ref[...] = _rms(x2, gfin_ref[...])


def _ffn(x1, seq, w, tm):
    n, d = x1.shape
    spt = seq // tm
    n_chunks, _, fc2 = w["w_up"].shape
    fc = fc2 // 2
    d_ff = n_chunks * fc
    hb = tm // HALO
    last = n // HALO - 1
    kern = functools.partial(_ffn_kernel, spt=spt, fc=fc)
    return pl.pallas_call(
        kern,
        grid=(n // tm,),
        in_specs=[pl.BlockSpec((tm, d), lambda i: (i, 0)),
                  pl.BlockSpec((HALO, d), lambda i: (jnp.maximum(i * hb - 1, 0), 0)),
                  pl.BlockSpec((HALO, d), lambda i: (jnp.minimum((i + 1) * hb, last), 0)),
                  _const_spec(w["g_ffn"].shape), _const_spec(w["w_up"].shape),
                  _const_spec(w["conv_w"].shape), _const_spec(w["conv_b"].shape),
                  _const_spec(w["w_down"].shape), _const_spec(w["g_final"].shape)],
        out_specs=pl.BlockSpec((tm, d), lambda i: (i, 0)),
        out_shape=jax.ShapeDtypeStruct((n, d), F32),
        scratch_shapes=[pltpu.VMEM((tm + 2 * HALO, d), BF16), pltpu.VMEM((tm, d_ff), BF16)],
        compiler_params=_params(("parallel",)),
        name="channel_mixer",
    )(x1, x1, x1, w["g_ffn"], w["w_up"], w["conv_w"], w["conv_b"], w["w_down"], w["g_final"])


FFN_CHUNK = 256


def _prep_weights(g_mix, w_in, g_q, w_uq, g_kv, w_ukv, w_attn_out, w_fourier_out, w_out,
                  g_ffn, w_up, conv_w, conv_b, w_down, g_final):
    q_rank = g_q.shape[0]
    kv_rank = g_kv.shape[0]
    d = w_in.shape[0]
    half = QK_ROPE // 2
    o_kr = q_rank + kv_rank
    o_f = o_kr + QK_ROPE
    kr = w_in[:, o_kr:o_f]
    kr_swapped = jnp.concatenate([kr[:, half:], kr[:, :half]], axis=1)
    kr_pad = jnp.zeros((d, LANES - 2 * QK_ROPE), F32)
    w_in_r = jnp.concatenate([w_in[:, :o_kr], kr, kr_swapped, kr_pad, w_in[:, o_f:]], axis=1)

    qh = w_uq.reshape(q_rank, N_HEADS, QK_NOPE + QK_ROPE)
    zq = jnp.zeros((q_rank, N_HEADS, HEAD_PAD - QK_NOPE - QK_ROPE), F32)
    w_q_plain = jnp.concatenate([qh, zq], axis=2)
    w_q_swap = jnp.concatenate([jnp.zeros((q_rank, N_HEADS, QK_NOPE), F32),
                                qh[:, :, QK_NOPE + half:], qh[:, :, QK_NOPE:QK_NOPE + half], zq], axis=2)
    w_q = jnp.concatenate([w_q_plain.reshape(q_rank, -1), w_q_swap.reshape(q_rank, -1)], axis=1)

    kvh = w_ukv.reshape(kv_rank, N_HEADS, QK_NOPE + V_DIM)
    w_k_top = jnp.concatenate([kvh[:, :, :QK_NOPE],
                               jnp.zeros((kv_rank, N_HEADS, HEAD_PAD - QK_NOPE), F32)], axis=2)
    w_v = jnp.concatenate([kvh[:, :, QK_NOPE:],
                           jnp.zeros((kv_rank, N_HEADS, HEAD_PAD - V_DIM), F32)], axis=2)
    eye = jnp.eye(QK_ROPE, dtype=F32)
    place = jnp.concatenate([jnp.zeros((QK_ROPE, QK_NOPE), F32), eye,
                             jnp.zeros((QK_ROPE, HEAD_PAD - QK_NOPE - QK_ROPE), F32)], axis=1)
    place = jnp.tile(place[:, None, :], (1, N_HEADS, 1))
    w_k_bot = jnp.concatenate([place, place,
                               jnp.zeros((LANES - 2 * QK_ROPE, N_HEADS, HEAD_PAD), F32)], axis=0)
    w_k = jnp.concatenate([w_k_top.reshape(kv_rank, -1), w_k_bot.reshape(LANES, -1)], axis=0)

    d_ff = w_down.shape[0]
    nc = d_ff // FFN_CHUNK
    gate_w = w_up[:, :d_ff].reshape(d, nc, FFN_CHUNK)
    val_w = w_up[:, d_ff:].reshape(d, nc, FFN_CHUNK)
    w_up_r = jnp.concatenate([gate_w, val_w], axis=2).transpose(1, 0, 2)
    k3 = conv_w.shape[0]
    cw_r = jnp.concatenate([conv_w[:, :d_ff].reshape(k3, nc, FFN_CHUNK),
                            conv_w[:, d_ff:].reshape(k3, nc, FFN_CHUNK)], axis=2).transpose(1, 0, 2)
    cb_r = jnp.concatenate([conv_b[:d_ff].reshape(nc, 1, FFN_CHUNK),
                            conv_b[d_ff:].reshape(nc, 1, FFN_CHUNK)], axis=2)
    return {
        "g_mix": g_mix[None, :], "w_in": w_in_r.astype(BF16),
        "g_q": g_q[None, :], "w_q": w_q.astype(BF16),
        "g_kv": g_kv[None, :], "w_v": w_v.reshape(kv_rank, -1).astype(BF16), "w_k": w_k.astype(BF16),
        "w_attn_out": w_attn_out.astype(BF16), "w_fourier_out": w_fourier_out.astype(BF16),
        "w_out": w_out.astype(BF16),
        "g_ffn": g_ffn[None, :], "w_up": w_up_r.astype(BF16), "conv_w": cw_r, "conv_b": cb_r,
        "w_down": w_down.astype(BF16), "g_final": g_final[None, :],
    }


def _dft_angles(rows, cols, period):
    prod = (rows.astype(np.int64) * cols.astype(np.int64)) % period
    return prod.astype(np.float64) * (2.0 * math.pi / period)


def _split_seq(seq):
    n1 = 1 << ((seq.bit_length() - 1 + 1) // 2)
    assert seq % n1 == 0
    return n1, seq // n1


def _tables(seq):
    half = QK_ROPE // 2
    scale = (QK_NOPE + QK_ROPE) ** -0.5 * math.log2(math.e)
    pos = jnp.arange(seq, dtype=F32)
    inv_freq = ROPE_THETA ** (-jnp.arange(0, QK_ROPE, 2, dtype=F32) / QK_ROPE)
    ang = pos[:, None] * inv_freq[None, :]
    cos, sin = jnp.cos(ang), jnp.sin(ang)
    ones = jnp.ones((seq, QK_NOPE), F32)
    zeros = lambda width: jnp.zeros((seq, width), F32)
    pad = HEAD_PAD - QK_NOPE - QK_ROPE
    ct = scale * jnp.concatenate([ones, cos, cos, zeros(pad)], axis=1)
    st = scale * jnp.concatenate([zeros(QK_NOPE), -sin, sin, zeros(pad)], axis=1)
    ck = jnp.concatenate([cos, cos, -sin, sin, zeros(LANES - 2 * QK_ROPE)], axis=1)

    as_bf16 = lambda table: jnp.asarray(table.astype(np.float32), dtype=BF16)
    ch = np.arange(F_GROUP_DIM)
    ang_c = _dft_angles(ch[:, None], ch[None, :], F_GROUP_DIM)
    dft_c = as_bf16(np.concatenate([np.cos(ang_c), -np.sin(ang_c)], axis=1))

    n1, n2 = _split_seq(seq)
    a = np.arange(n1)
    ang1 = _dft_angles(a[:, None], a[None, :], n1)
    c1, s1 = np.cos(ang1), np.sin(ang1)
    m1 = as_bf16(np.concatenate([np.concatenate([c1, s1], axis=1),
                                 np.concatenate([-s1, c1], axis=1)], axis=0))
    sp = (np.arange(n1)[:, None] + n1 * np.arange(n2)[None, :])
    ang2 = _dft_angles(sp[:, :, None], np.arange(n2)[None, None, :], seq)
    gtab = as_bf16(np.concatenate([np.cos(ang2), np.sin(ang2)], axis=2))
    return {"ct": ct, "st": st, "ck": ck, "dft_c": dft_c, "m1": m1, "gtab": gtab,
            "n1": n1, "n2": n2}


def _pick(seq, pref):
    t = pref
    while seq % t:
        t //= 2
    return t


def _trunk(x, w):
    batch, seq, d = x.shape
    x2d = x.reshape(batch * seq, d)
    tabs = _tables(seq)
    n1, n2 = tabs["n1"], tabs["n2"]
    tm = _pick(seq, 512)
    qt, kp, vt, yr, yi, gates = _inproj(x2d, seq, tabs, w, tm)
    ao = _attention(qt, kp, vt, batch, seq, tm)
    width = yr.shape[1]
    t = _dft1(yr, yi, tabs["m1"], batch, n1, n2)
    norm = 1.0 / math.sqrt(seq * F_GROUP_DIM)
    fo = _dft2(t, tabs["gtab"], batch, n1, n2, width, norm)
    big = _pick(seq, 1024)
    x1 = _merge(x2d, ao, fo, gates, w, big)
    y = _ffn(x1, seq, w, big)
    return y.reshape(batch, seq, d)


def kernel(x_prompt, x_sample, g_mix, w_in, g_q, w_uq, g_kv, w_ukv, w_attn_out, w_fourier_out,
           w_out, g_ffn, w_up, conv_w, conv_b, w_down, g_final):
    w = _prep_weights(g_mix[0], w_in[0], g_q[0], w_uq[0], g_kv[0], w_ukv[0], w_attn_out[0],
                      w_fourier_out[0], w_out[0], g_ffn[0], w_up[0], conv_w[0], conv_b[0],
                      w_down[0], g_final)
    return (_trunk(x_prompt, w), _trunk(x_sample, w))
```

```python
import functools
import math

import jax
import jax.numpy as jnp
import numpy as np
from jax import lax
from jax.experimental import pallas as pl
from jax.experimental.pallas import tpu as pltpu

F32 = jnp.float32
BF16 = jnp.bfloat16

N_HEADS = 8
QK_NOPE = 64
QK_ROPE = 32
V_DIM = 64
HEAD_PAD = 128
F_GROUPS = 4
F_GROUP_DIM = 128
ROPE_THETA = 10000.0
EPS = 1e-6

LANES = 128
VMEM_LIMIT = 56 * 1024 * 1024


def _params(sem, vmem=VMEM_LIMIT):
    return pltpu.CompilerParams(dimension_semantics=sem, vmem_limit_bytes=vmem)


def _const_spec(shape):
    nd = len(shape)
    return pl.BlockSpec(shape, lambda *_: (0,) * nd, pipeline_mode=pl.Buffered(1))


def _rms(v, g):
    r = lax.rsqrt(jnp.mean(v * v, axis=-1, keepdims=True) + EPS)
    return v * r * g


def _dot(a, b):
    return jnp.dot(a, b, preferred_element_type=F32)


def _inproj_kernel(x_ref, ct_ref, st_ref, ck_ref, gmix_ref, win_ref, gq_ref, wq_ref,
                   gkv_ref, wv_ref, wk_ref, dft_ref,
                   qt_ref, k_ref, vt_ref, yr_ref, yi_ref, gate_ref, *, q_rank, kv_rank, f_width):
    h = _rms(x_ref[...], gmix_ref[...]).astype(BF16)
    proj = _dot(h, win_ref[...])
    o_kv = q_rank
    o_kr = o_kv + kv_rank
    o_f = o_kr + LANES
    o_g = o_f + f_width
    cqn = _rms(proj[:, :o_kv], gq_ref[...]).astype(BF16)
    ckvn = _rms(proj[:, o_kv:o_kr], gkv_ref[...])

    q2 = _dot(cqn, wq_ref[...])
    ct = ct_ref[...]
    st = st_ref[...]
    hw = N_HEADS * HEAD_PAD
    q = jnp.concatenate(
        [q2[:, lo:lo + HEAD_PAD] * ct + q2[:, hw + lo:hw + lo + HEAD_PAD] * st
         for lo in range(0, hw, HEAD_PAD)], axis=1)
    qt_ref[...] = q.T.astype(BF16)

    lane = lax.broadcasted_iota(jnp.int32, (1, hw), 1)
    ones_col = (lane % HEAD_PAD == V_DIM).astype(F32)
    v = _dot(ckvn.astype(BF16), wv_ref[...]) + ones_col
    vt_ref[...] = v.T.astype(BF16)

    kr = proj[:, o_kr:o_f] * ck_ref[...]
    kin = jnp.concatenate([ckvn, kr], axis=1).astype(BF16)
    k_ref[...] = _dot(kin, wk_ref[...]).astype(BF16)

    xf = proj[:, o_f:o_g].astype(BF16)
    for g in range(F_GROUPS):
        lo = g * F_GROUP_DIM
        y = _dot(xf[:, lo:lo + F_GROUP_DIM], dft_ref[...])
        yr_ref[:, lo:lo + F_GROUP_DIM] = y[:, :F_GROUP_DIM].astype(BF16)
        yi_ref[:, lo:lo + F_GROUP_DIM] = y[:, F_GROUP_DIM:].astype(BF16)

    gate_ref[...] = jax.nn.sigmoid(proj[:, o_g:]).astype(BF16)


def _inproj(x2d, seq, tabs, w, tm):
    n, d = x2d.shape
    spt = seq // tm
    q_rank = w["g_q"].shape[1]
    kv_rank = w["g_kv"].shape[1]
    f_width = F_GROUPS * F_GROUP_DIM
    hw = N_HEADS * HEAD_PAD
    n_gate = w["w_in"].shape[1] - (q_rank + kv_rank + LANES + f_width)
    row = lambda width: pl.BlockSpec((tm, width), lambda i: (i, 0))
    tab = pl.BlockSpec((tm, LANES), lambda i: (i % spt, 0))
    tblk = pl.BlockSpec((None, None, hw, tm), lambda i: (i // spt, i % spt, 0, 0))
    t_shape = jax.ShapeDtypeStruct((n // seq, spt, hw, tm), BF16)
    kern = functools.partial(_inproj_kernel, q_rank=q_rank, kv_rank=kv_rank, f_width=f_width)
    return pl.pallas_call(
        kern,
        grid=(n // tm,),
        in_specs=[row(d), tab, tab, tab,
                  _const_spec(w["g_mix"].shape), _const_spec(w["w_in"].shape),
                  _const_spec(w["g_q"].shape), _const_spec(w["w_q"].shape),
                  _const_spec(w["g_kv"].shape), _const_spec(w["w_v"].shape),
                  _const_spec(w["w_k"].shape), _const_spec(tabs["dft_c"].shape)],
        out_specs=[tblk, row(hw), tblk, row(f_width), row(f_width), row(n_gate)],
        out_shape=[t_shape, jax.ShapeDtypeStruct((n, hw), BF16), t_shape,
                   jax.ShapeDtypeStruct((n, f_width), BF16),
                   jax.ShapeDtypeStruct((n, f_width), BF16), jax.ShapeDtypeStruct((n, n_gate), BF16)],
        compiler_params=_params(("parallel",)),
        name="inproj",
    )(x2d, tabs["ct"], tabs["st"], tabs["ck"], w["g_mix"], w["w_in"], w["g_q"], w["w_q"],
      w["g_kv"], w["w_v"], w["w_k"], tabs["dft_c"])


BF16_SUBLANES = 16
PV_ROWS = HEAD_PAD


def _attn_unroll(n_kv):
    return 8 if n_kv > 8 else 4


ATTN_Q_TILES = 8
ATTN_VMEM_BUDGET = 52 * 1024 * 1024


def _attn_kernel(qt_ref, k_ref, vt_ref, o_ref, m_ref, acc_ref, s_ref, smax_ref, *, n_kv):
    q_tiles, _, tile = qt_ref.shape
    unroll = s_ref.shape[0]
    trips = (n_kv - 1) // unroll
    tail = n_kv - unroll * trips
    carry_over = tail == unroll

    def scores(j, buf, qi):
        for hh in range(2):
            lo = hh * HEAD_PAD
            s = _dot(k_ref[j, :, lo:lo + HEAD_PAD], qt_ref[qi, lo:lo + HEAD_PAD, :])
            s_ref[buf, hh] = s
            smax_ref[buf, hh] = jnp.max(s, axis=0, keepdims=True)

    def one_query_tile(t, carry):
        m_ref[...] = jnp.full(m_ref.shape, -jnp.inf, F32)
        acc_ref[...] = jnp.zeros(acc_ref.shape, F32)

        def update(j, buf):
            for hh in range(2):
                lo = hh * HEAD_PAD
                m_prev = m_ref[hh]
                m_new = jnp.maximum(m_prev, smax_ref[buf, hh])
                p = jnp.exp2(s_ref[buf, hh] - m_new).astype(BF16)
                pv = _dot(vt_ref[j, lo:lo + PV_ROWS, :], p)
                acc_ref[hh] = jnp.exp2(m_prev - m_new) * acc_ref[hh] + pv
                m_ref[hh] = m_new

        def run(first, count, last):
            for u in range(count):
                if not (last and u == count - 1):
                    scores(first + u + 1, (u + 1) % unroll, t)
                elif carry_over:
                    scores(0, 0, jnp.minimum(t + 1, q_tiles - 1))
                update(first + u, u)

        def body(jj, inner):
            run(unroll * jj, unroll, False)
            return inner

        if not carry_over:
            scores(0, 0, t)
        lax.fori_loop(0, jnp.minimum(t + trips, trips), body, 0)
        run(unroll * trips, tail, True)
        o_t = jnp.concatenate(
            [acc_ref[hh, :V_DIM, :] / acc_ref[hh, V_DIM:V_DIM + 1, :] for hh in range(2)], axis=0)
        o_ref[pl.ds(pl.multiple_of(t * tile, tile), tile), :] = o_t.T.astype(o_ref.dtype)
        return carry

    if carry_over:
        scores(0, 0, 0)
    lax.fori_loop(0, q_tiles, one_query_tile, 0)


def _attention(qt, kp, vt, batch, seq, tile):
    n, hw = kp.shape
    nblk = seq // tile
    pair = 2 * HEAD_PAD
    k4 = kp.reshape(batch, nblk, tile, hw)
    fixed_bytes = (2 * 2 * seq * pair * 2
                   + _attn_unroll(nblk) * 2 * tile * tile * 4)
    per_q_tile_bytes = 2 * (pair * tile * 2 + tile * HEAD_PAD * 2)
    q_tiles = math.gcd(nblk, ATTN_Q_TILES)
    while q_tiles > 1 and fixed_bytes + q_tiles * per_q_tile_bytes > ATTN_VMEM_BUDGET:
        q_tiles //= 2
    nq = nblk // q_tiles
    kern = functools.partial(_attn_kernel, n_kv=nblk)
    return pl.pallas_call(
        kern,
        grid=(batch, N_HEADS // 2, nq),
        in_specs=[pl.BlockSpec((None, q_tiles, pair, tile), lambda b, hp, i: (b, i, hp, 0)),
                  pl.BlockSpec((None, nblk, tile, pair), lambda b, hp, i: (b, 0, 0, hp)),
                  pl.BlockSpec((None, nblk, pair, tile), lambda b, hp, i: (b, 0, hp, 0))],
        out_specs=pl.BlockSpec((q_tiles * tile, HEAD_PAD), lambda b, hp, i: (b * nq + i, hp)),
        out_shape=jax.ShapeDtypeStruct((n, N_HEADS * V_DIM), BF16),
        scratch_shapes=[pltpu.VMEM((2, 1, tile), F32), pltpu.VMEM((2, PV_ROWS, tile), F32),
                        pltpu.VMEM((_attn_unroll(nblk), 2, tile, tile), F32),
                        pltpu.VMEM((_attn_unroll(nblk), 2, 1, tile), F32)],
        compiler_params=_params(("parallel", "parallel", "arbitrary")),
        name="attention",
    )(qt, k4, vt)


DFT_ROWS = BF16_SUBLANES


def _dft1_kernel(yr_ref, yi_ref, m1_ref, t_ref, scr_ref):
    scr_ref[0] = yr_ref[...].astype(F32)
    scr_ref[1] = yi_ref[...].astype(F32)
    for b in range(DFT_ROWS):
        y = jnp.concatenate([scr_ref[0, :, b, :], scr_ref[1, :, b, :]], axis=0).astype(BF16)
        t_ref[b] = _dot(m1_ref[...], y).astype(BF16)


def _dft1(yr, yi, m1, batch, n1, n2):
    width = yr.shape[1]
    yr4 = yr.reshape(batch, n1, n2, width)
    yi4 = yi.reshape(batch, n1, n2, width)
    blk = pl.BlockSpec((None, n1, DFT_ROWS, width), lambda bt, j: (bt, 0, j, 0))
    return pl.pallas_call(
        _dft1_kernel,
        grid=(batch, n2 // DFT_ROWS),
        in_specs=[blk, blk, _const_spec(m1.shape)],
        out_specs=pl.BlockSpec((None, DFT_ROWS, 2 * n1, width), lambda bt, j: (bt, j, 0, 0)),
        out_shape=jax.ShapeDtypeStruct((batch, n2, 2 * n1, width), BF16),
        scratch_shapes=[pltpu.VMEM((2, n1, DFT_ROWS, width), F32)],
        compiler_params=_params(("parallel", "parallel")),
        name="seq_dft_stage1",
    )(yr4, yi4, m1)


def _dft2_kernel(tr_ref, ti_ref, g_ref, o_ref, scr_ref, oscr_ref, *, norm):
    scr_ref[0] = tr_ref[...].astype(F32)
    scr_ref[1] = ti_ref[...].astype(F32)
    for j in range(DFT_ROWS):
        t = jnp.concatenate([scr_ref[0, :, j, :], scr_ref[1, :, j, :]], axis=0).astype(BF16)
        oscr_ref[:, j, :] = _dot(g_ref[j], t) * norm
    o_ref[...] = oscr_ref[...].astype(o_ref.dtype)


def _dft2(t, gtab, batch, n1, n2, width, norm):
    cblks = n1 // DFT_ROWS
    kern = functools.partial(_dft2_kernel, norm=norm)
    out = pl.pallas_call(
        kern,
        grid=(batch, cblks),
        in_specs=[pl.BlockSpec((None, n2, DFT_ROWS, width), lambda bt, c: (bt, 0, c, 0)),
                  pl.BlockSpec((None, n2, DFT_ROWS, width), lambda bt, c: (bt, 0, cblks + c, 0)),
                  pl.BlockSpec((DFT_ROWS, n2, 2 * n2), lambda bt, c: (c, 0, 0))],
        out_specs=pl.BlockSpec((None, n2, DFT_ROWS, width), lambda bt, c: (bt, 0, c, 0)),
        out_shape=jax.ShapeDtypeStruct((batch, n2, n1, width), BF16),
        scratch_shapes=[pltpu.VMEM((2, n2, DFT_ROWS, width), F32),
                        pltpu.VMEM((n2, DFT_ROWS, width), F32)],
        compiler_params=_params(("parallel", "parallel")),
        name="seq_dft_stage2",
    )(t, t, gtab)
    return out.reshape(batch * n1 * n2, width)


def _merge_kernel(x_ref, ao_ref, f_ref, g_ref, wa_ref, wf_ref, wo_ref, o_ref):
    d = x_ref.shape[1]
    a = _dot(ao_ref[...], wa_ref[...])
    f = _dot(f_ref[...], wf_ref[...])
    g = g_ref[...].astype(F32)
    mix = (g[:, :d] * a + g[:, d:] * f).astype(BF16)
    o_ref[...] = x_ref[...] + _dot(mix, wo_ref[...])


def _merge(x2d, ao, fo, gates, w, tm):
    n, d = x2d.shape
    row = lambda width: pl.BlockSpec((tm, width), lambda i: (i, 0))
    return pl.pallas_call(
        _merge_kernel,
        grid=(n // tm,),
        in_specs=[row(d), row(ao.shape[1]), row(fo.shape[1]), row(gates.shape[1]),
                  _const_spec(w["w_attn_out"].shape), _const_spec(w["w_fourier_out"].shape),
                  _const_spec(w["w_out"].shape)],
        out_specs=row(d),
        out_shape=jax.ShapeDtypeStruct((n, d), F32),
        compiler_params=_params(("parallel",)),
        name="merge",
    )(x2d, ao, fo, gates, w["w_attn_out"], w["w_fourier_out"], w["w_out"])


HALO = 8


def _ffn_kernel(x_ref, xp_ref, xn_ref, gffn_ref, wup_ref, cw_ref, cb_ref, wdn_ref, gfin_ref,
                o_ref, hext_ref, act_ref, *, spt, fc):
    tm = x_ref.shape[0]
    n_chunks = wup_ref.shape[0]
    i = pl.program_id(0)
    has_prev = (i % spt != 0).astype(F32)
    has_next = (i % spt != spt - 1).astype(F32)
    g = gffn_ref[...]
    x = x_ref[...]
    hext_ref[0:HALO, :] = (_rms(xp_ref[...], g) * has_prev).astype(BF16)
    hext_ref[HALO:HALO + tm, :] = _rms(x, g).astype(BF16)
    hext_ref[HALO + tm:, :] = (_rms(xn_ref[...], g) * has_next).astype(BF16)
    hext = hext_ref[...]
    rows = tm + 2 * HALO
    for c in range(n_chunks):
        u = _dot(hext, wup_ref[c])
        cw = cw_ref[c]
        conv = (pltpu.roll(u, 1, axis=0) * cw[0:1] + u * cw[1:2]
                + pltpu.roll(u, rows - 1, axis=0) * cw[2:3] + cb_ref[c])
        conv = conv[HALO:HALO + tm]
        gate = conv[:, :fc]
        act_ref[:, c * fc:(c + 1) * fc] = (gate * jax.nn.sigmoid(gate) * conv[:, fc:]).astype(BF16)
    x2 = x + _dot(act_ref[...], wdn_ref[...])
    o_ref[...] = _rms(x2, gfin_ref[...])


def _ffn(x1, seq, w, tm):
    n, d = x1.shape
    spt = seq // tm
    n_chunks, _, fc2 = w["w_up"].shape
    fc = fc2 // 2
    d_ff = n_chunks * fc
    hb = tm // HALO
    last = n // HALO - 1
    kern = functools.partial(_ffn_kernel, spt=spt, fc=fc)
    return pl.pallas_call(
        kern,
        grid=(n // tm,),
        in_specs=[pl.BlockSpec((tm, d), lambda i: (i, 0)),
                  pl.BlockSpec((HALO, d), lambda i: (jnp.maximum(i * hb - 1, 0), 0)),
                  pl.BlockSpec((HALO, d), lambda i: (jnp.minimum((i + 1) * hb, last), 0)),
                  _const_spec(w["g_ffn"].shape), _const_spec(w["w_up"].shape),
                  _const_spec(w["conv_w"].shape), _const_spec(w["conv_b"].shape),
                  _const_spec(w["w_down"].shape), _const_spec(w["g_final"].shape)],
        out_specs=pl.BlockSpec((tm, d), lambda i: (i, 0)),
        out_shape=jax.ShapeDtypeStruct((n, d), F32),
        scratch_shapes=[pltpu.VMEM((tm + 2 * HALO, d), BF16), pltpu.VMEM((tm, d_ff), BF16)],
        compiler_params=_params(("parallel",)),
        name="channel_mixer",
    )(x1, x1, x1, w["g_ffn"], w["w_up"], w["conv_w"], w["conv_b"], w["w_down"], w["g_final"])


FFN_CHUNK = 256


def _prep_weights(g_mix, w_in, g_q, w_uq, g_kv, w_ukv, w_attn_out, w_fourier_out, w_out,
                  g_ffn, w_up, conv_w, conv_b, w_down, g_final):
    q_rank = g_q.shape[0]
    kv_rank = g_kv.shape[0]
    d = w_in.shape[0]
    half = QK_ROPE // 2
    o_kr = q_rank + kv_rank
    o_f = o_kr + QK_ROPE
    kr = w_in[:, o_kr:o_f]
    kr_swapped = jnp.concatenate([kr[:, half:], kr[:, :half]], axis=1)
    kr_pad = jnp.zeros((d, LANES - 2 * QK_ROPE), F32)
    w_in_r = jnp.concatenate([w_in[:, :o_kr], kr, kr_swapped, kr_pad, w_in[:, o_f:]], axis=1)

    qh = w_uq.reshape(q_rank, N_HEADS, QK_NOPE + QK_ROPE)
    zq = jnp.zeros((q_rank, N_HEADS, HEAD_PAD - QK_NOPE - QK_ROPE), F32)
    w_q_plain = jnp.concatenate([qh, zq], axis=2)
    w_q_swap = jnp.concatenate([jnp.zeros((q_rank, N_HEADS, QK_NOPE), F32),
                                qh[:, :, QK_NOPE + half:], qh[:, :, QK_NOPE:QK_NOPE + half], zq], axis=2)
    w_q = jnp.concatenate([w_q_plain.reshape(q_rank, -1), w_q_swap.reshape(q_rank, -1)], axis=1)

    kvh = w_ukv.reshape(kv_rank, N_HEADS, QK_NOPE + V_DIM)
    w_k_top = jnp.concatenate([kvh[:, :, :QK_NOPE],
                               jnp.zeros((kv_rank, N_HEADS, HEAD_PAD - QK_NOPE), F32)], axis=2)
    w_v = jnp.concatenate([kvh[:, :, QK_NOPE:],
                           jnp.zeros((kv_rank, N_HEADS, HEAD_PAD - V_DIM), F32)], axis=2)
    eye = jnp.eye(QK_ROPE, dtype=F32)
    place = jnp.concatenate([jnp.zeros((QK_ROPE, QK_NOPE), F32), eye,
                             jnp.zeros((QK_ROPE, HEAD_PAD - QK_NOPE - QK_ROPE), F32)], axis=1)
    place = jnp.tile(place[:, None, :], (1, N_HEADS, 1))
    w_k_bot = jnp.concatenate([place, place,
                               jnp.zeros((LANES - 2 * QK_ROPE, N_HEADS, HEAD_PAD), F32)], axis=0)
    w_k = jnp.concatenate([w_k_top.reshape(kv_rank, -1), w_k_bot.reshape(LANES, -1)], axis=0)

    d_ff = w_down.shape[0]
    nc = d_ff // FFN_CHUNK
    gate_w = w_up[:, :d_ff].reshape(d, nc, FFN_CHUNK)
    val_w = w_up[:, d_ff:].reshape(d, nc, FFN_CHUNK)
    w_up_r = jnp.concatenate([gate_w, val_w], axis=2).transpose(1, 0, 2)
    k3 = conv_w.shape[0]
    cw_r = jnp.concatenate([conv_w[:, :d_ff].reshape(k3, nc, FFN_CHUNK),
                            conv_w[:, d_ff:].reshape(k3, nc, FFN_CHUNK)], axis=2).transpose(1, 0, 2)
    cb_r = jnp.concatenate([conv_b[:d_ff].reshape(nc, 1, FFN_CHUNK),
                            conv_b[d_ff:].reshape(nc, 1, FFN_CHUNK)], axis=2)
    return {
        "g_mix": g_mix[None, :], "w_in": w_in_r.astype(BF16),
        "g_q": g_q[None, :], "w_q": w_q.astype(BF16),
        "g_kv": g_kv[None, :], "w_v": w_v.reshape(kv_rank, -1).astype(BF16), "w_k": w_k.astype(BF16),
        "w_attn_out": w_attn_out.astype(BF16), "w_fourier_out": w_fourier_out.astype(BF16),
        "w_out": w_out.astype(BF16),
        "g_ffn": g_ffn[None, :], "w_up": w_up_r.astype(BF16), "conv_w": cw_r, "conv_b": cb_r,
        "w_down": w_down.astype(BF16), "g_final": g_final[None, :],
    }


def _dft_angles(rows, cols, period):
    prod = (rows.astype(np.int64) * cols.astype(np.int64)) % period
    return prod.astype(np.float64) * (2.0 * math.pi / period)


def _split_seq(seq):
    n1 = 1 << ((seq.bit_length() - 1 + 1) // 2)
    assert seq % n1 == 0
    return n1, seq // n1


def _tables(seq):
    half = QK_ROPE // 2
    scale = (QK_NOPE + QK_ROPE) ** -0.5 * math.log2(math.e)
    pos = jnp.arange(seq, dtype=F32)
    inv_freq = ROPE_THETA ** (-jnp.arange(0, QK_ROPE, 2, dtype=F32) / QK_ROPE)
    ang = pos[:, None] * inv_freq[None, :]
    cos, sin = jnp.cos(ang), jnp.sin(ang)
    ones = jnp.ones((seq, QK_NOPE), F32)
    zeros = lambda width: jnp.zeros((seq, width), F32)
    pad = HEAD_PAD - QK_NOPE - QK_ROPE
    ct = scale * jnp.concatenate([ones, cos, cos, zeros(pad)], axis=1)
    st = scale * jnp.concatenate([zeros(QK_NOPE), -sin, sin, zeros(pad)], axis=1)
    ck = jnp.concatenate([cos, cos, -sin, sin, zeros(LANES - 2 * QK_ROPE)], axis=1)

    as_bf16 = lambda table: jnp.asarray(table.astype(np.float32)).astype(BF16)
    ch = np.arange(F_GROUP_DIM)
    ang_c = _dft_angles(ch[:, None], ch[None, :], F_GROUP_DIM)
    dft_c = as_bf16(np.concatenate([np.cos(ang_c), -np.sin(ang_c)], axis=1))

    n1, n2 = _split_seq(seq)
    a = np.arange(n1)
    ang1 = _dft_angles(a[:, None], a[None, :], n1)
    c1, s1 = np.cos(ang1), np.sin(ang1)
    m1 = as_bf16(np.concatenate([np.concatenate([c1, s1], axis=1),
                                 np.concatenate([-s1, c1], axis=1)], axis=0))
    sp = (np.arange(n1)[:, None] + n1 * np.arange(n2)[None, :])
    ang2 = _dft_angles(sp[:, :, None], np.arange(n2)[None, None, :], seq)
    gtab = as_bf16(np.concatenate([np.cos(ang2), np.sin(ang2)], axis=2))
    return {"ct": ct, "st": st, "ck": ck, "dft_c": dft_c, "m1": m1, "gtab": gtab,
            "n1": n1, "n2": n2}


def _pick(seq, pref):
    t = pref
    while seq % t:
        t //= 2
    return t


def _trunk(x, w):
    batch, seq, d = x.shape
    x2d = x.reshape(batch * seq, d)
    tabs = _tables(seq)
    n1, n2 = tabs["n1"], tabs["n2"]
    tm = _pick(seq, 512)
    qt, kp, vt, yr, yi, gates = _inproj(x2d, seq, tabs, w, tm)
    ao = _attention(qt, kp, vt, batch, seq, tm)
    width = yr.shape[1]
    t = _dft1(yr, yi, tabs["m1"], batch, n1, n2)
    norm = 1.0 / math.sqrt(seq * F_GROUP_DIM)
    fo = _dft2(t, tabs["gtab"], batch, n1, n2, width, norm)
    big = _pick(seq, 1024)
    x1 = _merge(x2d, ao, fo, gates, w, big)
    y = _ffn(x1, seq, w, big)
    return y.reshape(batch, seq, d)


def kernel(x_prompt, x_sample, g_mix, w_in, g_q, w_uq, g_kv, w_ukv, w_attn_out, w_fourier_out,
           w_out, g_ffn, w_up, conv_w, conv_b, w_down, g_final):
    w = _prep_weights(g_mix[0], w_in[0], g_q[0], w_uq[0], g_kv[0], w_ukv[0], w_attn_out[0],
                      w_fourier_out[0], w_out[0], g_ffn[0], w_up[0], conv_w[0], conv_b[0],
                      w_down[0], g_final)
    return (_trunk(x_prompt, w), _trunk(x_sample, w))
```
